```python
import jax, jax.numpy as jnp
from jax import lax
import numpy as np

D_MODEL = 1024
BATCH = 8
SEQ = 4096
DEPTH = 4

N_MIXERS = 2
N_MLA = (DEPTH + 1) // 2
N_SGU = DEPTH // 2
BRANCH_W = 2 * D_MODEL
MLA_HEADS = 16
NOPE_DIM = 128
ROPE_DIM = 64
V_DIM = BRANCH_W // MLA_HEADS
Q_LORA = 384
KV_LORA = 256
ROPE_THETA = 10000.0
Q_BLOCK = 128
ATTN_SCALE = (NOPE_DIM + ROPE_DIM) ** -0.5
MLA_IN_W = Q_LORA + KV_LORA + ROPE_DIM + BRANCH_W
CHUNK = 128
SGU_GROUPS = 16
SGU_GW = BRANCH_W // SGU_GROUPS
SGU_IN_W = 3 * BRANCH_W
EPS = 1e-6
LN_EPS = 1e-5

kernel_name = "hybrid_mla_chunked_sgu_trunk"


def rms_norm(x, g):
    xf = x.astype(jnp.float32)
    y = xf * lax.rsqrt(jnp.mean(xf * xf, axis=-1, keepdims=True) + EPS)
    return (y * g.astype(jnp.float32)).astype(x.dtype)


def layer_norm(x, g, b):
    xf = x.astype(jnp.float32)
    mu = jnp.mean(xf, axis=-1, keepdims=True)
    var = jnp.mean(jnp.square(xf - mu), axis=-1, keepdims=True)
    y = (xf - mu) * lax.rsqrt(var + LN_EPS)
    return (y * g.astype(jnp.float32) + b.astype(jnp.float32)).astype(x.dtype)


def rope_tables(positions, dtype):
    inv_freq = 1.0 / (ROPE_THETA ** (jnp.arange(0, ROPE_DIM, 2, dtype=jnp.float32) / ROPE_DIM))
    ang = positions.astype(jnp.float32)[..., None] * inv_freq
    return jnp.cos(ang).astype(dtype), jnp.sin(ang).astype(dtype)


def apply_rope(x, cos, sin):
    x1, x2 = jnp.split(x, 2, axis=-1)
    return jnp.concatenate([x1 * cos - x2 * sin, x2 * cos + x1 * sin], axis=-1)


def mla_branch(xn, w_in, q_norm_g, kv_norm_g, w_uq, w_ukv, cos, sin):
    B, S, _ = xn.shape
    h = xn @ w_in
    c_q, c_kv, k_r, gate = jnp.split(h, [Q_LORA, Q_LORA + KV_LORA, Q_LORA + KV_LORA + ROPE_DIM], axis=-1)
    c_q = rms_norm(c_q, q_norm_g)
    c_kv = rms_norm(c_kv, kv_norm_g)
    q = (c_q @ w_uq).reshape(B, S, MLA_HEADS, NOPE_DIM + ROPE_DIM)
    q_nope = q[..., :NOPE_DIM]
    q_rope = apply_rope(q[..., NOPE_DIM:], cos[:, :, None, :], sin[:, :, None, :])
    kv = (c_kv @ w_ukv).reshape(B, S, MLA_HEADS, NOPE_DIM + V_DIM)
    k_nope = kv[..., :NOPE_DIM]
    v = kv[..., NOPE_DIM:]
    k_rope = apply_rope(k_r, cos, sin)

    nb = S // Q_BLOCK
    qn_b = q_nope.reshape(B, nb, Q_BLOCK, MLA_HEADS, NOPE_DIM).transpose(1, 0, 2, 3, 4)
    qr_b = q_rope.reshape(B, nb, Q_BLOCK, MLA_HEADS, ROPE_DIM).transpose(1, 0, 2, 3, 4)
    k_pos = jnp.arange(S)
    neg = jnp.finfo(jnp.float32).min

    def attend(args):
        qn, qr, bi = args
        s = (jnp.einsum('bqhd,bkhd->bhqk', qn, k_nope, preferred_element_type=jnp.float32)
             + jnp.einsum('bqhr,bkr->bhqk', qr, k_rope, preferred_element_type=jnp.float32)) * ATTN_SCALE
        q_pos = bi * Q_BLOCK + jnp.arange(Q_BLOCK)
        causal = k_pos[None, :] <= q_pos[:, None]
        p = jax.nn.softmax(jnp.where(causal, s, neg), axis=-1).astype(v.dtype)
        return jnp.einsum('bhqk,bkhd->bqhd', p, v)

    o = lax.map(attend, (qn_b, qr_b, jnp.arange(nb)))
    o = o.transpose(1, 0, 2, 3, 4).reshape(B, S, MLA_HEADS * V_DIM)
    return o * jax.nn.silu(gate)


def sgu_branch(xn, w_in, ln_g, ln_b, w_s, b_s):
    B, S, _ = xn.shape
    u, v, gate = jnp.split(xn @ w_in, 3, axis=-1)
    u = jax.nn.gelu(u, approximate=False)
    v = layer_norm(jax.nn.gelu(v, approximate=False), ln_g, ln_b)
    vb = v.reshape(B, S // CHUNK, CHUNK, SGU_GROUPS, SGU_GW)
    tri = jnp.tril(jnp.ones((CHUNK, CHUNK), dtype=bool))
    w = jnp.where(tri[None], w_s, 0).astype(v.dtype)
    sv = jnp.einsum('gts,bcsgd->bctgd', w, vb) + b_s.T[None, None, :, :, None].astype(v.dtype)
    return u * sv.reshape(B, S, BRANCH_W) * jax.nn.silu(gate)


def setup_inputs(seed: int = 0) -> dict:
    key = jax.random.key(seed)
    ks = jax.random.split(key, 20)
    f32 = jnp.float32
    nrm = lambda k, shape, fan_in: jax.random.normal(k, shape, f32) * (fan_in ** -0.5)
    gain = lambda k, shape: 1.0 + 0.01 * jax.random.normal(k, shape, f32)
    x = jax.random.normal(ks[0], (BATCH, SEQ, D_MODEL), f32)
    positions = jnp.broadcast_to(jnp.arange(SEQ, dtype=jnp.int32)[None, :], (BATCH, SEQ))
    return {
        "x": x,
        "positions": positions,
        "norm_g": gain(ks[1], (DEPTH, D_MODEL)),
        "final_g": gain(ks[2], (D_MODEL,)),
        "mla_w_in": nrm(ks[3], (N_MLA, D_MODEL, MLA_IN_W), D_MODEL),
        "mla_q_norm_g": gain(ks[4], (N_MLA, Q_LORA)),
        "mla_kv_norm_g": gain(ks[5], (N_MLA, KV_LORA)),
        "mla_w_uq": nrm(ks[6], (N_MLA, Q_LORA, MLA_HEADS * (NOPE_DIM + ROPE_DIM)), Q_LORA),
        "mla_w_ukv": nrm(ks[7], (N_MLA, KV_LORA, MLA_HEADS * (NOPE_DIM + V_DIM)), KV_LORA),
        "mla_w_o": nrm(ks[8], (N_MLA, BRANCH_W, D_MODEL), BRANCH_W),
        "sgu_w_in": nrm(ks[9], (N_SGU, D_MODEL, SGU_IN_W), D_MODEL),
        "sgu_ln_g": gain(ks[10], (N_SGU, BRANCH_W)),
        "sgu_ln_b": 0.01 * jax.random.normal(ks[11], (N_SGU, BRANCH_W), f32),
        "sgu_w_s": nrm(ks[12], (N_SGU, SGU_GROUPS, CHUNK, CHUNK), CHUNK),
        "sgu_b_s": gain(ks[13], (N_SGU, SGU_GROUPS, CHUNK)),
        "sgu_w_o": nrm(ks[14], (N_SGU, BRANCH_W, D_MODEL), BRANCH_W),
    }


def reference(x, positions, norm_g, final_g, mla_w_in, mla_q_norm_g, mla_kv_norm_g,
              mla_w_uq, mla_w_ukv, mla_w_o, sgu_w_in, sgu_ln_g, sgu_ln_b, sgu_w_s,
              sgu_b_s, sgu_w_o):
    cos, sin = rope_tables(positions, x.dtype)
    for i in range(DEPTH):
        xn = rms_norm(x, norm_g[i])
        j = i // N_MIXERS
        if i % N_MIXERS == 0:
            y = mla_branch(xn, mla_w_in[j], mla_q_norm_g[j], mla_kv_norm_g[j],
                           mla_w_uq[j], mla_w_ukv[j], cos, sin) @ mla_w_o[j]
        else:
            y = sgu_branch(xn, sgu_w_in[j], sgu_ln_g[j], sgu_ln_b[j],
                           sgu_w_s[j], sgu_b_s[j]) @ sgu_w_o[j]
        x = x + y
    return rms_norm(x, final_g)
```

```python
import functools
import math

import jax
import jax.numpy as jnp
import numpy as np
from jax import lax
from jax.experimental import pallas as pl
from jax.experimental.pallas import tpu as pltpu

D_MODEL = 1024
BRANCH_W = 2 * D_MODEL
HEADS = 16
NOPE = 128
ROPE = 64
V_DIM = 128
Q_LORA = 384
KV_LORA = 256
ROPE_THETA = 10000.0
ATTN_SCALE = (NOPE + ROPE) ** -0.5
CHUNK = 128
GROUPS = 16
EPS = 1e-6
LN_EPS = 1e-5

LANE = 128
HEAD_PAD = 2 * LANE
Q_PRESCALE = ATTN_SCALE * math.log2(math.e)
MASK_VALUE = -1e30
VMEM_LIMIT = 56 * 1024 * 1024

F32 = jnp.float32
BF16 = jnp.bfloat16


def _params(n_axes):
    return pltpu.CompilerParams(
        dimension_semantics=("parallel",) * n_axes,
        vmem_limit_bytes=VMEM_LIMIT)


def _const_spec(shape):
    nd = len(shape)
    return pl.BlockSpec(shape, lambda *_: (0,) * nd)


def _rms(x, g):
    return x * lax.rsqrt(jnp.mean(x * x, axis=-1, keepdims=True) + EPS) * g


def _dot(a, b):
    return jnp.dot(a, b, preferred_element_type=F32)


def _rope_table_kernel(pos_ref, freq_ref, cc_ref, ss_ref):
    ang = pos_ref[...].astype(F32) * freq_ref[...]
    lane = lax.broadcasted_iota(jnp.int32, ang.shape, 1)
    cc_ref[...] = jnp.where(lane < ROPE, jnp.cos(ang), 0.0)
    sn = jnp.sin(ang)
    ss_ref[...] = jnp.where(lane < ROPE // 2, -sn, jnp.where(lane < ROPE, sn, 0.0))


def _rope_tables(positions, tm):
    T = positions.size
    inv_freq = 1.0 / (ROPE_THETA ** (jnp.arange(0, ROPE, 2, dtype=F32) / ROPE))
    freq = jnp.concatenate([inv_freq, inv_freq, jnp.zeros((LANE - ROPE,), F32)])[None, :]
    pos = positions.reshape(T, 1)
    return pl.pallas_call(
        _rope_table_kernel,
        grid=(T // tm,),
        in_specs=[pl.BlockSpec((tm, 1), lambda i: (i, 0)), _const_spec((1, LANE))],
        out_specs=[pl.BlockSpec((tm, LANE), lambda i: (i, 0))] * 2,
        out_shape=[jax.ShapeDtypeStruct((T, LANE), F32)] * 2,
        compiler_params=_params(1),
        name="rope_tables",
    )(pos, freq)


def _rope(r, cc, ss):
    swapped = pltpu.roll(r, LANE - ROPE // 2, axis=1) + pltpu.roll(r, ROPE // 2, axis=1)
    return r * cc + swapped * ss


_CQ0 = BRANCH_W
_CKV0 = _CQ0 + Q_LORA
_KR0 = _CKV0 + KV_LORA
MLA_IN_PAD = _KR0 + LANE


def _mla_in_kernel(x_ref, g_ref, win_ref, gq_ref, gkv_ref, wuq_ref, wukv_ref,
                   cc_ref, ss_ref, q_ref, kn_ref, v_ref, kr_ref, sg_ref):
    xn = _rms(x_ref[...], g_ref[...]).astype(BF16)
    h = _dot(xn, win_ref[...])
    gate = h[:, :BRANCH_W]
    sg_ref[...] = (gate * jax.nn.sigmoid(gate)).astype(BF16)
    cc = cc_ref[...]
    ss = ss_ref[...]
    kr_ref[...] = _rope(h[:, _KR0:], cc, ss).astype(BF16)

    cq = _rms(h[:, _CQ0:_CKV0], gq_ref[...] * Q_PRESCALE).astype(BF16)
    q = _dot(cq, wuq_ref[...])
    for hd in range(HEADS):
        c0 = hd * HEAD_PAD
        q_ref[:, c0:c0 + LANE] = q[:, c0:c0 + LANE].astype(BF16)
        q_ref[:, c0 + LANE:c0 + HEAD_PAD] = _rope(
            q[:, c0 + LANE:c0 + HEAD_PAD], cc, ss).astype(BF16)

    ckv = _rms(h[:, _CKV0:_KR0], gkv_ref[...]).astype(BF16)
    kv = _dot(ckv, wukv_ref[...])
    kn_ref[...] = kv[:, :BRANCH_W].astype(BF16)
    v_ref[...] = kv[:, BRANCH_W:].astype(BF16)


def _mla_in(x2, g, win, gq, gkv, wuq, wukv, cc, ss, tm):
    T = x2.shape[0]
    row = lambda w: pl.BlockSpec((tm, w), lambda i: (i, 0))
    return pl.pallas_call(
        _mla_in_kernel,
        grid=(T // tm,),
        in_specs=[row(D_MODEL), _const_spec(g.shape), _const_spec(win.shape),
                  _const_spec(gq.shape), _const_spec(gkv.shape),
                  _const_spec(wuq.shape), _const_spec(wukv.shape),
                  row(LANE), row(LANE)],
        out_specs=[row(HEADS * HEAD_PAD), row(BRANCH_W), row(BRANCH_W), row(LANE),
                   row(BRANCH_W)],
        out_shape=[jax.ShapeDtypeStruct((T, HEADS * HEAD_PAD), BF16),
                   jax.ShapeDtypeStruct((T, BRANCH_W), BF16),
                   jax.ShapeDtypeStruct((T, BRANCH_W), BF16),
                   jax.ShapeDtypeStruct((T, LANE), BF16),
                   jax.ShapeDtypeStruct((T, BRANCH_W), BF16)],
        compiler_params=_params(1),
        name="mla_in",
    )(x2, g, win, gq, gkv, wuq, wukv, cc, ss)


def _flash_kernel(q_ref, kn_ref, kr_ref, v_ref, sg_ref, o_ref, *, seq, tile):
    nq = seq // tile

    def step(q, k0, carry, masked):
        m, l, acc = carry
        k = jnp.concatenate([kn_ref[0, pl.ds(k0, tile), :],
                             kr_ref[0, pl.ds(k0, tile), :]], axis=1)
        s = lax.dot_general(q, k, (((1,), (1,)), ((), ())),
                            preferred_element_type=F32)
        if masked:
            row = lax.broadcasted_iota(jnp.int32, s.shape, 0)
            col = lax.broadcasted_iota(jnp.int32, s.shape, 1)
            s = jnp.where(col <= row, s, MASK_VALUE)
        m_new = jnp.maximum(m, jnp.max(s, axis=-1, keepdims=True))
        alpha = jnp.exp2(m - m_new)
        p = jnp.exp2(s - m_new)
        l = alpha * l + jnp.sum(p, axis=-1, keepdims=True)
        acc = alpha * acc + _dot(p.astype(BF16), v_ref[0, pl.ds(k0, tile), :])
        return m_new, l, acc

    def q_body(qi, _):
        q0 = pl.multiple_of(qi * tile, tile)
        q = q_ref[0, pl.ds(q0, tile), :]
        init = (jnp.full((tile, 1), MASK_VALUE, F32), jnp.zeros((tile, 1), F32),
                jnp.zeros((tile, V_DIM), F32))
        carry = lax.fori_loop(
            0, qi,
            lambda kj, c: step(q, pl.multiple_of(kj * tile, tile), c, False),
            init)
        _, l, acc = step(q, q0, carry, True)
        o = acc / l * sg_ref[0, pl.ds(q0, tile), :].astype(F32)
        o_ref[0, pl.ds(q0, tile), :] = o.astype(BF16)
        return 0

    lax.fori_loop(0, nq, q_body, 0)


def _flash(q, kn, kr, v, sg, tile):
    B, S, _ = q.shape
    head = lambda w: pl.BlockSpec((1, S, w), lambda b, h: (b, 0, h))
    return pl.pallas_call(
        functools.partial(_flash_kernel, seq=S, tile=tile),
        grid=(B, HEADS),
        in_specs=[head(HEAD_PAD), head(LANE),
                  pl.BlockSpec((1, S, LANE), lambda b, h: (b, 0, 0)),
                  head(V_DIM), head(V_DIM)],
        out_specs=head(V_DIM),
        out_shape=jax.ShapeDtypeStruct((B, S, BRANCH_W), BF16),
        compiler_params=_params(2),
        name="flash",
    )(q, kn, kr, v, sg)


def _out_proj_kernel(a_ref, w_ref, x_ref, o_ref):
    o_ref[...] = x_ref[...] + _dot(a_ref[...], w_ref[...])


def _out_proj(a, w, x2, tm):
    T = x2.shape[0]
    return pl.pallas_call(
        _out_proj_kernel,
        grid=(T // tm,),
        in_specs=[pl.BlockSpec((tm, BRANCH_W), lambda i: (i, 0)),
                  _const_spec(w.shape),
                  pl.BlockSpec((tm, D_MODEL), lambda i: (i, 0))],
        out_specs=pl.BlockSpec((tm, D_MODEL), lambda i: (i, 0)),
        out_shape=jax.ShapeDtypeStruct((T, D_MODEL), F32),
        compiler_params=_params(1),
        name="out_proj",
    )(a, w, x2)


def _gelu(x):
    return 0.5 * x * (1.0 + lax.erf(x * np.float32(math.sqrt(0.5))))


def _sgu_kernel(x_ref, g_ref, win_ref, lng_ref, lnb_ref, ws_ref, bs_ref, wo_ref,
                fg_ref, o_ref, vln_ref, ug_ref, prod_ref, *, tm, final_norm):
    x = x_ref[...]
    xn = _rms(x, g_ref[...]).astype(BF16)

    gv = _gelu(_dot(xn, win_ref[:, BRANCH_W:2 * BRANCH_W]))
    mu = jnp.mean(gv, axis=-1, keepdims=True)
    d = gv - mu
    var = jnp.mean(d * d, axis=-1, keepdims=True)
    vln_ref[...] = (d * lax.rsqrt(var + LN_EPS) * lng_ref[...] + lnb_ref[...]).astype(BF16)

    ug_ref[...] = _gelu(_dot(xn, win_ref[:, :BRANCH_W]))
    gate = _dot(xn, win_ref[:, 2 * BRANCH_W:])
    ug_ref[...] = ug_ref[...] * (gate * jax.nn.sigmoid(gate))

    nc = tm // CHUNK
    row = lax.broadcasted_iota(jnp.int32, (CHUNK, CHUNK), 0)
    col = lax.broadcasted_iota(jnp.int32, (CHUNK, CHUNK), 1)
    tri = col <= row
    for g in range(GROUPS):
        c0 = g * LANE
        wm = jnp.where(tri, ws_ref[g], 0.0).astype(BF16)
        rhs = jnp.concatenate(
            [vln_ref[c * CHUNK:(c + 1) * CHUNK, c0:c0 + LANE] for c in range(nc)], axis=1)
        res = _dot(wm, rhs)
        bias = bs_ref[:, c0:c0 + LANE]
        for c in range(nc):
            sv = res[:, c * LANE:(c + 1) * LANE] + bias
            r0 = c * CHUNK
            prod_ref[r0:r0 + CHUNK, c0:c0 + LANE] = (
                ug_ref[r0:r0 + CHUNK, c0:c0 + LANE] * sv).astype(BF16)

    y = x + _dot(prod_ref[...], wo_ref[...])
    if final_norm:
        y = _rms(y, fg_ref[...])
    o_ref[...] = y


def _sgu_layer(x2, g, win, lng, lnb, ws, bs_full, wo, fg, tm, final_norm):
    T = x2.shape[0]
    return pl.pallas_call(
        functools.partial(_sgu_kernel, tm=tm, final_norm=final_norm),
        grid=(T // tm,),
        in_specs=[pl.BlockSpec((tm, D_MODEL), lambda i: (i, 0)),
                  _const_spec(g.shape), _const_spec(win.shape), _const_spec(lng.shape),
                  _const_spec(lnb.shape), _const_spec(ws.shape), _const_spec(bs_full.shape),
                  _const_spec(wo.shape), _const_spec(fg.shape)],
        out_specs=pl.BlockSpec((tm, D_MODEL), lambda i: (i, 0)),
        out_shape=jax.ShapeDtypeStruct((T, D_MODEL), F32),
        scratch_shapes=[pltpu.VMEM((tm, BRANCH_W), BF16),
                        pltpu.VMEM((tm, BRANCH_W), F32),
                        pltpu.VMEM((tm, BRANCH_W), BF16)],
        compiler_params=_params(1),
        name="sgu_layer",
    )(x2, g, win, lng, lnb, ws, bs_full, wo, fg)


def _prep_mla_weights(w_in, w_uq, w_ukv):
    cq, ckv, kr, gate = jnp.split(w_in, [Q_LORA, Q_LORA + KV_LORA, Q_LORA + KV_LORA + ROPE], axis=1)
    win = jnp.concatenate(
        [gate, cq, ckv, kr, jnp.zeros((D_MODEL, LANE - ROPE), w_in.dtype)], axis=1).astype(BF16)
    wq = w_uq.reshape(Q_LORA, HEADS, NOPE + ROPE)
    wq = jnp.pad(wq, ((0, 0), (0, 0), (0, HEAD_PAD - NOPE - ROPE)))
    wq = wq.reshape(Q_LORA, HEADS * HEAD_PAD).astype(BF16)
    wkv = w_ukv.reshape(KV_LORA, HEADS, NOPE + V_DIM)
    wkv = jnp.concatenate([wkv[:, :, :NOPE].reshape(KV_LORA, HEADS * NOPE),
                           wkv[:, :, NOPE:].reshape(KV_LORA, HEADS * V_DIM)], axis=1).astype(BF16)
    return win, wq, wkv


def kernel(x, positions, norm_g, final_g, mla_w_in, mla_q_norm_g, mla_kv_norm_g, mla_w_uq,
           mla_w_ukv, mla_w_o, sgu_w_in, sgu_ln_g, sgu_ln_b, sgu_w_s, sgu_b_s, sgu_w_o):
    B, S, D = x.shape
    T = B * S
    depth = norm_g.shape[0]
    assert D == D_MODEL and S % 512 == 0

    x2 = x.reshape(T, D)
    cc, ss = _rope_tables(positions, 512)
    fg = final_g[None, :]

    for i in range(depth):
        g = norm_g[i][None, :]
        j = i // 2
        if i % 2 == 0:
            win, wq, wkv = _prep_mla_weights(mla_w_in[j], mla_w_uq[j], mla_w_ukv[j])
            q, kn, v, kr, sg = _mla_in(x2, g, win, mla_q_norm_g[j][None, :],
                                       mla_kv_norm_g[j][None, :], wq, wkv, cc, ss, 256)
            shp = lambda a: a.reshape(B, S, a.shape[-1])
            o = _flash(shp(q), shp(kn), shp(kr), shp(v), shp(sg), 512)
            x2 = _out_proj(o.reshape(T, BRANCH_W), mla_w_o[j].astype(BF16), x2, 512)
            assert i != depth - 1
        else:
            bs_full = jnp.repeat(sgu_b_s[j].T, LANE, axis=1)
            x2 = _sgu_layer(x2, g, sgu_w_in[j].astype(BF16), sgu_ln_g[j][None, :],
                            sgu_ln_b[j][None, :], sgu_w_s[j], bs_full,
                            sgu_w_o[j].astype(BF16), fg, 256, i == depth - 1)
    return x2.reshape(B, S, D)
```

```python
import functools
import math

import jax
import jax.numpy as jnp
import numpy as np
from jax import lax
from jax.experimental import pallas as pl
from jax.experimental.pallas import tpu as pltpu

D_MODEL = 1024
BRANCH_W = 2 * D_MODEL
HEADS = 16
NOPE = 128
ROPE = 64
V_DIM = 128
Q_LORA = 384
KV_LORA = 256
ROPE_THETA = 10000.0
ATTN_SCALE = (NOPE + ROPE) ** -0.5
CHUNK = 128
GROUPS = 16
EPS = 1e-6
LN_EPS = 1e-5

LANE = 128
HEAD_PAD = 2 * LANE
HEADS_PER_STEP = 2
SUM_ROWS = 16
Q_PRESCALE = ATTN_SCALE * math.log2(math.e)
MASK_VALUE = -1e30
VMEM_LIMIT = 56 * 1024 * 1024

F32 = jnp.float32
BF16 = jnp.bfloat16
NT_DIMS = (((1,), (1,)), ((), ()))


def _params(n_axes):
    return pltpu.CompilerParams(
        dimension_semantics=("parallel",) * n_axes,
        vmem_limit_bytes=VMEM_LIMIT)


def _const_spec(shape):
    nd = len(shape)
    return pl.BlockSpec(shape, lambda *_: (0,) * nd)


def _rms(x, g):
    return x * lax.rsqrt(jnp.mean(x * x, axis=-1, keepdims=True) + EPS) * g


def _dot(a, b):
    return jnp.dot(a, b, preferred_element_type=F32)


def _dot_nt(a, b):
    return lax.dot_general(a, b, NT_DIMS, preferred_element_type=F32)


def _rope_table_kernel(pos_ref, freq_ref, cc_ref, ss_ref):
    ang = pos_ref[...].astype(F32) * freq_ref[...]
    lane = lax.broadcasted_iota(jnp.int32, ang.shape, 1)
    cc_ref[...] = jnp.where(lane < ROPE, jnp.cos(ang), 0.0)
    sn = jnp.sin(ang)
    ss_ref[...] = jnp.where(lane < ROPE // 2, -sn, jnp.where(lane < ROPE, sn, 0.0))


def _rope_tables(positions, tm):
    T = positions.size
    inv_freq = 1.0 / (ROPE_THETA ** (jnp.arange(0, ROPE, 2, dtype=F32) / ROPE))
    freq = jnp.concatenate([inv_freq, inv_freq, jnp.zeros((LANE - ROPE,), F32)])[None, :]
    pos = positions.reshape(T, 1)
    return pl.pallas_call(
        _rope_table_kernel,
        grid=(T // tm,),
        in_specs=[pl.BlockSpec((tm, 1), lambda i: (i, 0)), _const_spec((1, LANE))],
        out_specs=[pl.BlockSpec((tm, LANE), lambda i: (i, 0))] * 2,
        out_shape=[jax.ShapeDtypeStruct((T, LANE), F32)] * 2,
        compiler_params=_params(1),
        name="rope_tables",
    )(pos, freq)


def _rope(r, cc, ss):
    swapped = pltpu.roll(r, LANE - ROPE // 2, axis=1) + pltpu.roll(r, ROPE // 2, axis=1)
    return r * cc + swapped * ss


_CQ0 = BRANCH_W
_CKV0 = _CQ0 + Q_LORA
_KR0 = _CKV0 + KV_LORA


def _mla_in_kernel(x_ref, g_ref, win_ref, gq_ref, gkv_ref, wuq_ref, wuk_ref, wuvt_ref,
                   cc_ref, ss_ref, q_ref, kn_ref, vt_ref, kr_ref, sg_ref):
    xn = _rms(x_ref[...], g_ref[...]).astype(BF16)
    h = _dot(xn, win_ref[...])
    gate = h[:, :BRANCH_W]
    sg_ref[...] = (gate * jax.nn.sigmoid(gate)).astype(BF16)
    cc = cc_ref[...]
    ss = ss_ref[...]
    kr_ref[...] = _rope(h[:, _KR0:], cc, ss).astype(BF16)

    cq = _rms(h[:, _CQ0:_CKV0], gq_ref[...] * Q_PRESCALE).astype(BF16)
    q = _dot(cq, wuq_ref[...])
    for hd in range(HEADS):
        c0 = hd * HEAD_PAD
        q_ref[:, c0:c0 + LANE] = q[:, c0:c0 + LANE].astype(BF16)
        q_ref[:, c0 + LANE:c0 + HEAD_PAD] = _rope(
            q[:, c0 + LANE:c0 + HEAD_PAD], cc, ss).astype(BF16)

    ckv = _rms(h[:, _CKV0:_KR0], gkv_ref[...]).astype(BF16)
    kn_ref[...] = _dot(ckv, wuk_ref[...]).astype(BF16)
    vt_ref[0] = _dot_nt(wuvt_ref[...], ckv).astype(BF16)


def _mla_in(x2, g, win, gq, gkv, wuq, wuk, wuvt, cc, ss, batch, tm):
    T = x2.shape[0]
    S = T // batch
    tiles_per_seq = S // tm
    row = lambda w: pl.BlockSpec((tm, w), lambda i: (i, 0))
    return pl.pallas_call(
        _mla_in_kernel,
        grid=(T // tm,),
        in_specs=[row(D_MODEL), _const_spec(g.shape), _const_spec(win.shape),
                  _const_spec(gq.shape), _const_spec(gkv.shape),
                  _const_spec(wuq.shape), _const_spec(wuk.shape), _const_spec(wuvt.shape),
                  row(LANE), row(LANE)],
        out_specs=[row(HEADS * HEAD_PAD), row(BRANCH_W),
                   pl.BlockSpec((1, BRANCH_W, tm),
                                lambda i: (i // tiles_per_seq, 0, i % tiles_per_seq)),
                   row(LANE), row(BRANCH_W)],
        out_shape=[jax.ShapeDtypeStruct((T, HEADS * HEAD_PAD), BF16),
                   jax.ShapeDtypeStruct((T, BRANCH_W), BF16),
                   jax.ShapeDtypeStruct((batch, BRANCH_W, S), BF16),
                   jax.ShapeDtypeStruct((T, LANE), BF16),
                   jax.ShapeDtypeStruct((T, BRANCH_W), BF16)],
        compiler_params=_params(1),
        name="mla_in",
    )(x2, g, win, gq, gkv, wuq, wuk, wuvt, cc, ss)


def _flash_kernel(q_ref, kn_ref, kr_ref, vt_ref, sg_ref, o_ref,
                  st_ref, pt_ref, acc_ref, mask_ref, *, seq, tile):
    nq = seq // tile
    heads = range(HEADS_PER_STEP)

    kpos = lax.broadcasted_iota(jnp.int32, (tile, tile), 0)
    qpos = lax.broadcasted_iota(jnp.int32, (tile, tile), 1)
    mask_ref[...] = jnp.where(kpos <= qpos, 0.0, MASK_VALUE)

    def scores(hd, q0, k0):
        q = q_ref[0, pl.ds(q0, tile), hd * HEAD_PAD:(hd + 1) * HEAD_PAD]
        k = jnp.concatenate([kn_ref[0, pl.ds(k0, tile), hd * LANE:(hd + 1) * LANE],
                             kr_ref[0, pl.ds(k0, tile), :]], axis=1)
        st_ref[hd] = _dot_nt(k, q)

    def softmax(hd, st, m):
        m_new = jnp.maximum(m, jnp.max(st, axis=0, keepdims=True))
        pt_ref[hd] = jnp.exp2(st - m_new).astype(BF16)
        return m_new, jnp.exp2(m - m_new)

    def accumulate(hd, k0, alpha):
        vt = vt_ref[0, hd * V_DIM:(hd + 1) * V_DIM, pl.ds(k0, tile)]
        vt = jnp.concatenate([vt, jnp.ones((SUM_ROWS, tile), BF16)], axis=0)
        acc_ref[hd] = alpha * acc_ref[hd] + _dot(vt, pt_ref[hd])

    def tile_start(i):
        return pl.multiple_of(i * tile, tile)

    def finalize(qi):
        q0 = tile_start(qi)
        for hd in heads:
            acc = acc_ref[hd]
            o = (acc[:V_DIM] / acc[V_DIM:V_DIM + 1]).T
            sg = sg_ref[0, pl.ds(q0, tile), hd * V_DIM:(hd + 1) * V_DIM].astype(F32)
            o_ref[0, pl.ds(q0, tile), hd * V_DIM:(hd + 1) * V_DIM] = (o * sg).astype(BF16)

    for hd in heads:
        pt_ref[hd] = jnp.zeros((tile, tile), BF16)
        acc_ref[hd] = jnp.zeros((V_DIM + SUM_ROWS, tile), F32)
        scores(hd, 0, 0)

    def step(t, carry):
        qi, kj, stats = carry
        is_diag = kj == qi
        is_first = kj == 0
        k_prev = tile_start(jnp.maximum(jnp.where(is_first, qi - 1, kj - 1), 0))
        q_next = tile_start(jnp.where(is_diag, jnp.minimum(qi + 1, nq - 1), qi))
        k_next = tile_start(jnp.where(is_diag, 0, kj + 1))

        def body(masked):
            def run(stats):
                st = [st_ref[hd] for hd in heads]
                for hd in heads:
                    accumulate(hd, k_prev, stats[hd][1])
                for hd in heads:
                    scores(hd, q_next, k_next)
                out = []
                for hd in heads:
                    m = jnp.where(is_first, MASK_VALUE, stats[hd][0])
                    s = st[hd] + mask_ref[...] if masked else st[hd]
                    out.append(softmax(hd, s, m))
                return tuple(out)
            return run

        stats = lax.cond(is_diag, body(True), body(False), stats)

        @pl.when(jnp.logical_and(is_first, t > 0))
        def _():
            finalize(qi - 1)

        return (jnp.where(is_diag, qi + 1, qi), jnp.where(is_diag, 0, kj + 1), stats)

    init = tuple((jnp.full((1, tile), MASK_VALUE, F32), jnp.ones((1, tile), F32))
                 for _ in heads)
    _, _, stats = lax.fori_loop(0, nq * (nq + 1) // 2, step,
                                (jnp.int32(0), jnp.int32(0), init))
    for hd in heads:
        accumulate(hd, tile_start(nq - 1), stats[hd][1])
    finalize(nq - 1)


def _flash(q, kn, kr, vt, sg, tile):
    B, S, _ = q.shape
    n = HEADS_PER_STEP
    heads = lambda w: pl.BlockSpec((1, S, n * w), lambda b, h: (b, 0, h))
    return pl.pallas_call(
        functools.partial(_flash_kernel, seq=S, tile=tile),
        grid=(B, HEADS // n),
        in_specs=[heads(HEAD_PAD), heads(LANE),
                  pl.BlockSpec((1, S, LANE), lambda b, h: (b, 0, 0)),
                  pl.BlockSpec((1, n * V_DIM, S), lambda b, h: (b, h, 0)),
                  heads(V_DIM)],
        out_specs=heads(V_DIM),
        out_shape=jax.ShapeDtypeStruct((B, S, BRANCH_W), BF16),
        scratch_shapes=[pltpu.VMEM((n, tile, tile), F32),
                        pltpu.VMEM((n, tile, tile), BF16),
                        pltpu.VMEM((n, V_DIM + SUM_ROWS, tile), F32),
                        pltpu.VMEM((tile, tile), F32)],
        compiler_params=_params(2),
        name="flash",
    )(q, kn, kr, vt, sg)


def _out_proj_kernel(a_ref, w_ref, x_ref, fg_ref, o_ref, *, final_norm):
    y = x_ref[...] + _dot(a_ref[...], w_ref[...])
    if final_norm:
        y = _rms(y, fg_ref[...])
    o_ref[...] = y


def _out_proj(a, w, x2, fg, tm, final_norm):
    T = x2.shape[0]
    return pl.pallas_call(
        functools.partial(_out_proj_kernel, final_norm=final_norm),
        grid=(T // tm,),
        in_specs=[pl.BlockSpec((tm, BRANCH_W), lambda i: (i, 0)),
                  _const_spec(w.shape),
                  pl.BlockSpec((tm, D_MODEL), lambda i: (i, 0)),
                  _const_spec(fg.shape)],
        out_specs=pl.BlockSpec((tm, D_MODEL), lambda i: (i, 0)),
        out_shape=jax.ShapeDtypeStruct((T, D_MODEL), F32),
        compiler_params=_params(1),
        name="out_proj",
    )(a, w, x2, fg)


def _gelu(x):
    return 0.5 * x * (1.0 + lax.erf(x * np.float32(math.sqrt(0.5))))


def _sgu_kernel(x_ref, g_ref, win_ref, lng_ref, lnb_ref, ws_ref, bs_ref, wo_ref,
                fg_ref, o_ref, vln_ref, ug_ref, prod_ref, *, tm, final_norm):
    x = x_ref[...]
    xn = _rms(x, g_ref[...]).astype(BF16)

    gv = _gelu(_dot(xn, win_ref[:, BRANCH_W:2 * BRANCH_W]))
    mu = jnp.mean(gv, axis=-1, keepdims=True)
    d = gv - mu
    var = jnp.mean(d * d, axis=-1, keepdims=True)
    vln_ref[...] = (d * lax.rsqrt(var + LN_EPS) * lng_ref[...] + lnb_ref[...]).astype(BF16)

    ug_ref[...] = _gelu(_dot(xn, win_ref[:, :BRANCH_W]))
    gate = _dot(xn, win_ref[:, 2 * BRANCH_W:])
    ug_ref[...] = ug_ref[...] * (gate * jax.nn.sigmoid(gate))

    nc = tm // CHUNK
    row = lax.broadcasted_iota(jnp.int32, (CHUNK, CHUNK), 0)
    col = lax.broadcasted_iota(jnp.int32, (CHUNK, CHUNK), 1)
    tri = col <= row
    for g in range(GROUPS):
        c0 = g * LANE
        wm = jnp.where(tri, ws_ref[g], 0.0).astype(BF16)
        rhs = jnp.concatenate(
            [vln_ref[c * CHUNK:(c + 1) * CHUNK, c0:c0 + LANE] for c in range(nc)], axis=1)
        res = _dot(wm, rhs)
        bias = bs_ref[:, c0:c0 + LANE]
        for c in range(nc):
            sv = res[:, c * LANE:(c + 1) * LANE] + bias
            r0 = c * CHUNK
            prod_ref[r0:r0 + CHUNK, c0:c0 + LANE] = (
                ug_ref[r0:r0 + CHUNK, c0:c0 + LANE] * sv).astype(BF16)

    y = x + _dot(prod_ref[...], wo_ref[...])
    if final_norm:
        y = _rms(y, fg_ref[...])
    o_ref[...] = y


def _sgu_layer(x2, g, win, lng, lnb, ws, bs_full, wo, fg, tm, final_norm):
    T = x2.shape[0]
    return pl.pallas_call(
        functools.partial(_sgu_kernel, tm=tm, final_norm=final_norm),
        grid=(T // tm,),
        in_specs=[pl.BlockSpec((tm, D_MODEL), lambda i: (i, 0)),
                  _const_spec(g.shape), _const_spec(win.shape), _const_spec(lng.shape),
                  _const_spec(lnb.shape), _const_spec(ws.shape), _const_spec(bs_full.shape),
                  _const_spec(wo.shape), _const_spec(fg.shape)],
        out_specs=pl.BlockSpec((tm, D_MODEL), lambda i: (i, 0)),
        out_shape=jax.ShapeDtypeStruct((T, D_MODEL), F32),
        scratch_shapes=[pltpu.VMEM((tm, BRANCH_W), BF16),
                        pltpu.VMEM((tm, BRANCH_W), F32),
                        pltpu.VMEM((tm, BRANCH_W), BF16)],
        compiler_params=_params(1),
        name="sgu_layer",
    )(x2, g, win, lng, lnb, ws, bs_full, wo, fg)


def _prep_mla_weights(w_in, w_uq, w_ukv):
    cq, ckv, kr, gate = jnp.split(w_in, [Q_LORA, Q_LORA + KV_LORA, Q_LORA + KV_LORA + ROPE], axis=1)
    win = jnp.concatenate(
        [gate, cq, ckv, kr, jnp.zeros((D_MODEL, LANE - ROPE), w_in.dtype)], axis=1).astype(BF16)
    wq = w_uq.reshape(Q_LORA, HEADS, NOPE + ROPE)
    wq = jnp.pad(wq, ((0, 0), (0, 0), (0, HEAD_PAD - NOPE - ROPE)))
    wq = wq.reshape(Q_LORA, HEADS * HEAD_PAD).astype(BF16)
    wkv = w_ukv.reshape(KV_LORA, HEADS, NOPE + V_DIM)
    wuk = wkv[:, :, :NOPE].reshape(KV_LORA, HEADS * NOPE).astype(BF16)
    wuvt = wkv[:, :, NOPE:].reshape(KV_LORA, HEADS * V_DIM).T.astype(BF16)
    return win, wq, wuk, wuvt


def kernel(x, positions, norm_g, final_g, mla_w_in, mla_q_norm_g, mla_kv_norm_g, mla_w_uq,
           mla_w_ukv, mla_w_o, sgu_w_in, sgu_ln_g, sgu_ln_b, sgu_w_s, sgu_b_s, sgu_w_o):
    B, S, D = x.shape
    T = B * S
    depth = norm_g.shape[0]
    assert D == D_MODEL and S % 512 == 0

    x2 = x.reshape(T, D)
    cc, ss = _rope_tables(positions, 512)
    fg = final_g[None, :]

    for i in range(depth):
        g = norm_g[i][None, :]
        j = i // 2
        last = i == depth - 1
        if i % 2 == 0:
            win, wq, wuk, wuvt = _prep_mla_weights(mla_w_in[j], mla_w_uq[j], mla_w_ukv[j])
            q, kn, vt, kr, sg = _mla_in(x2, g, win, mla_q_norm_g[j][None, :],
                                        mla_kv_norm_g[j][None, :], wq, wuk, wuvt, cc, ss, B, 256)
            shp = lambda a: a.reshape(B, S, a.shape[-1])
            o = _flash(shp(q), shp(kn), shp(kr), vt, shp(sg), 512)
            x2 = _out_proj(o.reshape(T, BRANCH_W), mla_w_o[j].astype(BF16), x2, fg, 512, last)
        else:
            bs_full = jnp.repeat(sgu_b_s[j].T, LANE, axis=1)
            x2 = _sgu_layer(x2, g, sgu_w_in[j].astype(BF16), sgu_ln_g[j][None, :],
                            sgu_ln_b[j][None, :], sgu_w_s[j], bs_full,
                            sgu_w_o[j].astype(BF16), fg, 256, last)
    return x2.reshape(B, S, D)
```

```python
import functools
import math

import jax
import jax.numpy as jnp
import numpy as np
from jax import lax
from jax.experimental import pallas as pl
from jax.experimental.pallas import tpu as pltpu

D_MODEL = 1024
BRANCH_W = 2 * D_MODEL
HEADS = 16
NOPE = 128
ROPE = 64
V_DIM = 128
Q_LORA = 384
KV_LORA = 256
ROPE_THETA = 10000.0
ATTN_SCALE = (NOPE + ROPE) ** -0.5
CHUNK = 128
GROUPS = 16
EPS = 1e-6
LN_EPS = 1e-5

LANE = 128
HEAD_PAD = 2 * LANE
HEADS_PER_STEP = 2
SUM_ROWS = 16
Q_PRESCALE = ATTN_SCALE * math.log2(math.e)
MASK_VALUE = -1e30
VMEM_LIMIT = 56 * 1024 * 1024

F32 = jnp.float32
BF16 = jnp.bfloat16
NT_DIMS = (((1,), (1,)), ((), ()))


def _params(n_axes):
    return pltpu.CompilerParams(
        dimension_semantics=("parallel",) * n_axes,
        vmem_limit_bytes=VMEM_LIMIT)


def _const_spec(shape):
    nd = len(shape)
    return pl.BlockSpec(shape, lambda *_: (0,) * nd)


def _rms(x, g):
    return x * lax.rsqrt(jnp.mean(x * x, axis=-1, keepdims=True) + EPS) * g


def _dot(a, b):
    return jnp.dot(a, b, preferred_element_type=F32)


def _dot_nt(a, b):
    return lax.dot_general(a, b, NT_DIMS, preferred_element_type=F32)


def _rope_table_kernel(pos_ref, freq_ref, cc_ref, ss_ref):
    ang = pos_ref[...].astype(F32) * freq_ref[...]
    lane = lax.broadcasted_iota(jnp.int32, ang.shape, 1)
    cc_ref[...] = jnp.where(lane < ROPE, jnp.cos(ang), 0.0)
    sn = jnp.sin(ang)
    ss_ref[...] = jnp.where(lane < ROPE // 2, -sn, jnp.where(lane < ROPE, sn, 0.0))


def _rope_tables(positions, tm):
    T = positions.size
    inv_freq = 1.0 / (ROPE_THETA ** (jnp.arange(0, ROPE, 2, dtype=F32) / ROPE))
    freq = jnp.concatenate([inv_freq, inv_freq, jnp.zeros((LANE - ROPE,), F32)])[None, :]
    pos = positions.reshape(T, 1)
    return pl.pallas_call(
        _rope_table_kernel,
        grid=(T // tm,),
        in_specs=[pl.BlockSpec((tm, 1), lambda i: (i, 0)), _const_spec((1, LANE))],
        out_specs=[pl.BlockSpec((tm, LANE), lambda i: (i, 0))] * 2,
        out_shape=[jax.ShapeDtypeStruct((T, LANE), F32)] * 2,
        compiler_params=_params(1),
        name="rope_tables",
    )(pos, freq)


def _rope(r, cc, ss):
    swapped = pltpu.roll(r, LANE - ROPE // 2, axis=1) + pltpu.roll(r, ROPE // 2, axis=1)
    return r * cc + swapped * ss


_CQ0 = BRANCH_W
_CKV0 = _CQ0 + Q_LORA
_KR0 = _CKV0 + KV_LORA


def _mla_in_kernel(x_ref, g_ref, win_ref, gq_ref, gkv_ref, wuq_ref, wuk_ref, wuvt_ref,
                   cc_ref, ss_ref, q_ref, kn_ref, vt_ref, kr_ref, sg_ref):
    xn = _rms(x_ref[...], g_ref[...]).astype(BF16)
    h = _dot(xn, win_ref[...])
    gate = h[:, :BRANCH_W]
    sg_ref[...] = (gate * jax.nn.sigmoid(gate)).astype(BF16)
    cc = cc_ref[...]
    ss = ss_ref[...]
    kr_ref[...] = _rope(h[:, _KR0:], cc, ss).astype(BF16)

    cq = _rms(h[:, _CQ0:_CKV0], gq_ref[...] * Q_PRESCALE).astype(BF16)
    q = _dot(cq, wuq_ref[...])
    for hd in range(HEADS):
        c0 = hd * HEAD_PAD
        q_ref[:, c0:c0 + LANE] = q[:, c0:c0 + LANE].astype(BF16)
        q_ref[:, c0 + LANE:c0 + HEAD_PAD] = _rope(
            q[:, c0 + LANE:c0 + HEAD_PAD], cc, ss).astype(BF16)

    ckv = _rms(h[:, _CKV0:_KR0], gkv_ref[...]).astype(BF16)
    kn_ref[...] = _dot(ckv, wuk_ref[...]).astype(BF16)
    vt_ref[0] = _dot_nt(wuvt_ref[...], ckv).astype(BF16)


def _mla_in(x2, g, win, gq, gkv, wuq, wuk, wuvt, cc, ss, batch, tm):
    T = x2.shape[0]
    S = T // batch
    tiles_per_seq = S // tm
    row = lambda w: pl.BlockSpec((tm, w), lambda i: (i, 0))
    return pl.pallas_call(
        _mla_in_kernel,
        grid=(T // tm,),
        in_specs=[row(D_MODEL), _const_spec(g.shape), _const_spec(win.shape),
                  _const_spec(gq.shape), _const_spec(gkv.shape),
                  _const_spec(wuq.shape), _const_spec(wuk.shape), _const_spec(wuvt.shape),
                  row(LANE), row(LANE)],
        out_specs=[row(HEADS * HEAD_PAD), row(BRANCH_W),
                   pl.BlockSpec((1, BRANCH_W, tm),
                                lambda i: (i // tiles_per_seq, 0, i % tiles_per_seq)),
                   row(LANE), row(BRANCH_W)],
        out_shape=[jax.ShapeDtypeStruct((T, HEADS * HEAD_PAD), BF16),
                   jax.ShapeDtypeStruct((T, BRANCH_W), BF16),
                   jax.ShapeDtypeStruct((batch, BRANCH_W, S), BF16),
                   jax.ShapeDtypeStruct((T, LANE), BF16),
                   jax.ShapeDtypeStruct((T, BRANCH_W), BF16)],
        compiler_params=_params(1),
        name="mla_in",
    )(x2, g, win, gq, gkv, wuq, wuk, wuvt, cc, ss)


def _flash_kernel(q_ref, kn_ref, kr_ref, vt_ref, sg_ref, o_ref,
                  st_ref, pt_ref, acc_ref, mask_ref, *, seq, tile):
    nq = seq // tile
    heads = range(HEADS_PER_STEP)

    kpos = lax.broadcasted_iota(jnp.int32, (tile, tile), 0)
    qpos = lax.broadcasted_iota(jnp.int32, (tile, tile), 1)
    mask_ref[...] = jnp.where(kpos <= qpos, 0.0, MASK_VALUE)

    def scores(hd, q0, k0, masked):
        q = q_ref[0, pl.ds(q0, tile), hd * HEAD_PAD:(hd + 1) * HEAD_PAD]
        k = jnp.concatenate([kn_ref[0, pl.ds(k0, tile), hd * LANE:(hd + 1) * LANE],
                             kr_ref[0, pl.ds(k0, tile), :]], axis=1)
        st = _dot_nt(k, q)
        if masked:
            st = st + mask_ref[...]
        st_ref[hd] = st
        return jnp.max(st, axis=0, keepdims=True)

    def softmax(hd, st, m, m_tile):
        m_new = jnp.maximum(m, m_tile)
        pt_ref[hd] = jnp.exp2(st - m_new).astype(BF16)
        return m_new, jnp.exp2(m - m_new)

    def accumulate(hd, k0, alpha):
        vt = vt_ref[0, hd * V_DIM:(hd + 1) * V_DIM, pl.ds(k0, tile)]
        vt = jnp.concatenate([vt, jnp.ones((SUM_ROWS, tile), BF16)], axis=0)
        acc_ref[hd] = alpha * acc_ref[hd] + _dot(vt, pt_ref[hd])

    def tile_start(i):
        return pl.multiple_of(i * tile, tile)

    def finalize(qi):
        q0 = tile_start(qi)
        for hd in heads:
            acc = acc_ref[hd]
            o = (acc[:V_DIM] / acc[V_DIM:V_DIM + 1]).T
            sg = sg_ref[0, pl.ds(q0, tile), hd * V_DIM:(hd + 1) * V_DIM].astype(F32)
            o_ref[0, pl.ds(q0, tile), hd * V_DIM:(hd + 1) * V_DIM] = (o * sg).astype(BF16)

    init = []
    for hd in heads:
        pt_ref[hd] = jnp.zeros((tile, tile), BF16)
        acc_ref[hd] = jnp.zeros((V_DIM + SUM_ROWS, tile), F32)
        init.append((jnp.full((1, tile), MASK_VALUE, F32), jnp.ones((1, tile), F32),
                     scores(hd, 0, 0, True)))

    def step(t, carry):
        qi, kj, stats = carry
        is_diag = kj == qi
        is_first = kj == 0
        k_prev = tile_start(jnp.maximum(jnp.where(is_first, qi - 1, kj - 1), 0))
        q_next = tile_start(jnp.where(is_diag, jnp.minimum(qi + 1, nq - 1), qi))
        k_next = tile_start(jnp.where(is_diag, 0, kj + 1))

        def body(next_masked):
            def run(stats):
                st = [st_ref[hd] for hd in heads]
                for hd in heads:
                    accumulate(hd, k_prev, stats[hd][1])
                m_next = [scores(hd, q_next, k_next, next_masked) for hd in heads]
                out = []
                for hd in heads:
                    m = jnp.where(is_first, MASK_VALUE, stats[hd][0])
                    out.append(softmax(hd, st[hd], m, stats[hd][2]) + (m_next[hd],))
                return tuple(out)
            return run

        stats = lax.cond(kj + 1 == qi, body(True), body(False), stats)

        @pl.when(jnp.logical_and(is_first, t > 0))
        def _():
            finalize(qi - 1)

        return (jnp.where(is_diag, qi + 1, qi), jnp.where(is_diag, 0, kj + 1), stats)

    _, _, stats = lax.fori_loop(0, nq * (nq + 1) // 2, step,
                                (jnp.int32(0), jnp.int32(0), tuple(init)))
    for hd in heads:
        accumulate(hd, tile_start(nq - 1), stats[hd][1])
    finalize(nq - 1)


def _flash(q, kn, kr, vt, sg, tile):
    B, S, _ = q.shape
    n = HEADS_PER_STEP
    heads = lambda w: pl.BlockSpec((1, S, n * w), lambda b, h: (b, 0, h))
    return pl.pallas_call(
        functools.partial(_flash_kernel, seq=S, tile=tile),
        grid=(B, HEADS // n),
        in_specs=[heads(HEAD_PAD), heads(LANE),
                  pl.BlockSpec((1, S, LANE), lambda b, h: (b, 0, 0)),
                  pl.BlockSpec((1, n * V_DIM, S), lambda b, h: (b, h, 0)),
                  heads(V_DIM)],
        out_specs=heads(V_DIM),
        out_shape=jax.ShapeDtypeStruct((B, S, BRANCH_W), BF16),
        scratch_shapes=[pltpu.VMEM((n, tile, tile), F32),
                        pltpu.VMEM((n, tile, tile), BF16),
                        pltpu.VMEM((n, V_DIM + SUM_ROWS, tile), F32),
                        pltpu.VMEM((tile, tile), F32)],
        compiler_params=_params(2),
        name="flash",
    )(q, kn, kr, vt, sg)


def _out_proj_kernel(a_ref, w_ref, x_ref, fg_ref, o_ref, *, final_norm):
    y = x_ref[...] + _dot(a_ref[...], w_ref[...])
    if final_norm:
        y = _rms(y, fg_ref[...])
    o_ref[...] = y


def _out_proj(a, w, x2, fg, tm, final_norm):
    T = x2.shape[0]
    return pl.pallas_call(
        functools.partial(_out_proj_kernel, final_norm=final_norm),
        grid=(T // tm,),
        in_specs=[pl.BlockSpec((tm, BRANCH_W), lambda i: (i, 0)),
                  _const_spec(w.shape),
                  pl.BlockSpec((tm, D_MODEL), lambda i: (i, 0)),
                  _const_spec(fg.shape)],
        out_specs=pl.BlockSpec((tm, D_MODEL), lambda i: (i, 0)),
        out_shape=jax.ShapeDtypeStruct((T, D_MODEL), F32),
        compiler_params=_params(1),
        name="out_proj",
    )(a, w, x2, fg)


def _gelu(x):
    return 0.5 * x * (1.0 + lax.erf(x * np.float32(math.sqrt(0.5))))


def _sgu_kernel(x_ref, g_ref, win_ref, lng_ref, lnb_ref, ws_ref, bs_ref, wo_ref,
                fg_ref, o_ref, vln_ref, ug_ref, prod_ref, *, tm, final_norm):
    x = x_ref[...]
    xn = _rms(x, g_ref[...]).astype(BF16)

    gv = _gelu(_dot(xn, win_ref[:, BRANCH_W:2 * BRANCH_W]))
    mu = jnp.mean(gv, axis=-1, keepdims=True)
    d = gv - mu
    var = jnp.mean(d * d, axis=-1, keepdims=True)
    vln_ref[...] = (d * lax.rsqrt(var + LN_EPS) * lng_ref[...] + lnb_ref[...]).astype(BF16)

    ug_ref[...] = _gelu(_dot(xn, win_ref[:, :BRANCH_W]))
    gate = _dot(xn, win_ref[:, 2 * BRANCH_W:])
    ug_ref[...] = ug_ref[...] * (gate * jax.nn.sigmoid(gate))

    nc = tm // CHUNK
    row = lax.broadcasted_iota(jnp.int32, (CHUNK, CHUNK), 0)
    col = lax.broadcasted_iota(jnp.int32, (CHUNK, CHUNK), 1)
    tri = col <= row
    for g in range(GROUPS):
        c0 = g * LANE
        wm = jnp.where(tri, ws_ref[g], 0.0).astype(BF16)
        rhs = jnp.concatenate(
            [vln_ref[c * CHUNK:(c + 1) * CHUNK, c0:c0 + LANE] for c in range(nc)], axis=1)
        res = _dot(wm, rhs)
        bias = bs_ref[:, c0:c0 + LANE]
        for c in range(nc):
            sv = res[:, c * LANE:(c + 1) * LANE] + bias
            r0 = c * CHUNK
            prod_ref[r0:r0 + CHUNK, c0:c0 + LANE] = (
                ug_ref[r0:r0 + CHUNK, c0:c0 + LANE] * sv).astype(BF16)

    y = x + _dot(prod_ref[...], wo_ref[...])
    if final_norm:
        y = _rms(y, fg_ref[...])
    o_ref[...] = y


def _sgu_layer(x2, g, win, lng, lnb, ws, bs_full, wo, fg, tm, final_norm):
    T = x2.shape[0]
    return pl.pallas_call(
        functools.partial(_sgu_kernel, tm=tm, final_norm=final_norm),
        grid=(T // tm,),
        in_specs=[pl.BlockSpec((tm, D_MODEL), lambda i: (i, 0)),
                  _const_spec(g.shape), _const_spec(win.shape), _const_spec(lng.shape),
                  _const_spec(lnb.shape), _const_spec(ws.shape), _const_spec(bs_full.shape),
                  _const_spec(wo.shape), _const_spec(fg.shape)],
        out_specs=pl.BlockSpec((tm, D_MODEL), lambda i: (i, 0)),
        out_shape=jax.ShapeDtypeStruct((T, D_MODEL), F32),
        scratch_shapes=[pltpu.VMEM((tm, BRANCH_W), BF16),
                        pltpu.VMEM((tm, BRANCH_W), F32),
                        pltpu.VMEM((tm, BRANCH_W), BF16)],
        compiler_params=_params(1),
        name="sgu_layer",
    )(x2, g, win, lng, lnb, ws, bs_full, wo, fg)


def _prep_mla_weights(w_in, w_uq, w_ukv):
    cq, ckv, kr, gate = jnp.split(w_in, [Q_LORA, Q_LORA + KV_LORA, Q_LORA + KV_LORA + ROPE], axis=1)
    win = jnp.concatenate(
        [gate, cq, ckv, kr, jnp.zeros((D_MODEL, LANE - ROPE), w_in.dtype)], axis=1).astype(BF16)
    wq = w_uq.reshape(Q_LORA, HEADS, NOPE + ROPE)
    wq = jnp.pad(wq, ((0, 0), (0, 0), (0, HEAD_PAD - NOPE - ROPE)))
    wq = wq.reshape(Q_LORA, HEADS * HEAD_PAD).astype(BF16)
    wkv = w_ukv.reshape(KV_LORA, HEADS, NOPE + V_DIM)
    wuk = wkv[:, :, :NOPE].reshape(KV_LORA, HEADS * NOPE).astype(BF16)
    wuvt = wkv[:, :, NOPE:].reshape(KV_LORA, HEADS * V_DIM).T.astype(BF16)
    return win, wq, wuk, wuvt


def kernel(x, positions, norm_g, final_g, mla_w_in, mla_q_norm_g, mla_kv_norm_g, mla_w_uq,
           mla_w_ukv, mla_w_o, sgu_w_in, sgu_ln_g, sgu_ln_b, sgu_w_s, sgu_b_s, sgu_w_o):
    B, S, D = x.shape
    T = B * S
    depth = norm_g.shape[0]
    assert D == D_MODEL and S % 512 == 0

    x2 = x.reshape(T, D)
    cc, ss = _rope_tables(positions, 512)
    fg = final_g[None, :]

    for i in range(depth):
        g = norm_g[i][None, :]
        j = i // 2
        last = i == depth - 1
        if i % 2 == 0:
            win, wq, wuk, wuvt = _prep_mla_weights(mla_w_in[j], mla_w_uq[j], mla_w_ukv[j])
            q, kn, vt, kr, sg = _mla_in(x2, g, win, mla_q_norm_g[j][None, :],
                                        mla_kv_norm_g[j][None, :], wq, wuk, wuvt, cc, ss, B, 256)
            shp = lambda a: a.reshape(B, S, a.shape[-1])
            o = _flash(shp(q), shp(kn), shp(kr), vt, shp(sg), 512)
            x2 = _out_proj(o.reshape(T, BRANCH_W), mla_w_o[j].astype(BF16), x2, fg, 512, last)
        else:
            bs_full = jnp.repeat(sgu_b_s[j].T, LANE, axis=1)
            x2 = _sgu_layer(x2, g, sgu_w_in[j].astype(BF16), sgu_ln_g[j][None, :],
                            sgu_ln_b[j][None, :], sgu_w_s[j], bs_full,
                            sgu_w_o[j].astype(BF16), fg, 256, last)
    return x2.reshape(B, S, D)
```

```python
import functools
import math

import jax
import jax.numpy as jnp
import numpy as np
from jax import lax
from jax.experimental import pallas as pl
from jax.experimental.pallas import tpu as pltpu

D_MODEL = 1024
BRANCH_W = 2 * D_MODEL
HEADS = 16
NOPE = 128
ROPE = 64
V_DIM = 128
Q_LORA = 384
KV_LORA = 256
ROPE_THETA = 10000.0
ATTN_SCALE = (NOPE + ROPE) ** -0.5
CHUNK = 128
GROUPS = 16
EPS = 1e-6
LN_EPS = 1e-5

LANE = 128
HEAD_PAD = 2 * LANE
HEADS_PER_STEP = 4
SUM_ROWS = 16
Q_PRESCALE = ATTN_SCALE * math.log2(math.e)
MASK_VALUE = -1e30
VMEM_LIMIT = 56 * 1024 * 1024

F32 = jnp.float32
BF16 = jnp.bfloat16
NT_DIMS = (((1,), (1,)), ((), ()))


def _params(n_axes):
    return pltpu.CompilerParams(
        dimension_semantics=("parallel",) * n_axes,
        vmem_limit_bytes=VMEM_LIMIT)


def _const_spec(shape):
    nd = len(shape)
    return pl.BlockSpec(shape, lambda *_: (0,) * nd, pipeline_mode=pl.Buffered(1))


def _rms(x, g):
    return x * lax.rsqrt(jnp.mean(x * x, axis=-1, keepdims=True) + EPS) * g


def _dot(a, b):
    return jnp.dot(a, b, preferred_element_type=F32)


def _dot_nt(a, b):
    return lax.dot_general(a, b, NT_DIMS, preferred_element_type=F32)


def _rope_table_kernel(pos_ref, freq_ref, cc_ref, ss_ref):
    ang = pos_ref[...].astype(F32) * freq_ref[...]
    lane = lax.broadcasted_iota(jnp.int32, ang.shape, 1)
    cc_ref[...] = jnp.where(lane < ROPE, jnp.cos(ang), 0.0)
    sn = jnp.sin(ang)
    ss_ref[...] = jnp.where(lane < ROPE // 2, -sn, jnp.where(lane < ROPE, sn, 0.0))


def _rope_tables(positions, tm):
    T = positions.size
    inv_freq = 1.0 / (ROPE_THETA ** (jnp.arange(0, ROPE, 2, dtype=F32) / ROPE))
    freq = jnp.concatenate([inv_freq, inv_freq, jnp.zeros((LANE - ROPE,), F32)])[None, :]
    pos = positions.reshape(T, 1)
    return pl.pallas_call(
        _rope_table_kernel,
        grid=(T // tm,),
        in_specs=[pl.BlockSpec((tm, 1), lambda i: (i, 0)), _const_spec((1, LANE))],
        out_specs=[pl.BlockSpec((tm, LANE), lambda i: (i, 0))] * 2,
        out_shape=[jax.ShapeDtypeStruct((T, LANE), F32)] * 2,
        compiler_params=_params(1),
        name="rope_tables",
    )(pos, freq)


def _rope(r, cc, ss):
    swapped = pltpu.roll(r, LANE - ROPE // 2, axis=1) + pltpu.roll(r, ROPE // 2, axis=1)
    return r * cc + swapped * ss


_CQ0 = BRANCH_W
_CKV0 = _CQ0 + Q_LORA
_KR0 = _CKV0 + KV_LORA


def _mla_in_kernel(x_ref, g_ref, win_ref, gq_ref, gkv_ref, wuq_ref, wuk_ref, wuvt_ref,
                   cc_ref, ss_ref, q_ref, kn_ref, vt_ref, kr_ref, sg_ref):
    xn = _rms(x_ref[...], g_ref[...]).astype(BF16)
    h = _dot(xn, win_ref[...])
    gate = h[:, :BRANCH_W]
    sg_ref[...] = (gate * jax.nn.sigmoid(gate)).astype(BF16)
    cc = cc_ref[...]
    ss = ss_ref[...]
    kr_ref[...] = _rope(h[:, _KR0:], cc, ss).astype(BF16)

    cq = _rms(h[:, _CQ0:_CKV0], gq_ref[...] * Q_PRESCALE).astype(BF16)
    q = _dot(cq, wuq_ref[...])
    for hd in range(HEADS):
        c0 = hd * HEAD_PAD
        q_ref[:, c0:c0 + LANE] = q[:, c0:c0 + LANE].astype(BF16)
        q_ref[:, c0 + LANE:c0 + HEAD_PAD] = _rope(
            q[:, c0 + LANE:c0 + HEAD_PAD], cc, ss).astype(BF16)

    ckv = _rms(h[:, _CKV0:_KR0], gkv_ref[...]).astype(BF16)
    kn_ref[...] = _dot(ckv, wuk_ref[...]).astype(BF16)
    vt_ref[0] = _dot_nt(wuvt_ref[...], ckv).astype(BF16)


def _mla_in(x2, g, win, gq, gkv, wuq, wuk, wuvt, cc, ss, batch, tm):
    T = x2.shape[0]
    S = T // batch
    tiles_per_seq = S // tm
    row = lambda w: pl.BlockSpec((tm, w), lambda i: (i, 0))
    return pl.pallas_call(
        _mla_in_kernel,
        grid=(T // tm,),
        in_specs=[row(D_MODEL), _const_spec(g.shape), _const_spec(win.shape),
                  _const_spec(gq.shape), _const_spec(gkv.shape),
                  _const_spec(wuq.shape), _const_spec(wuk.shape), _const_spec(wuvt.shape),
                  row(LANE), row(LANE)],
        out_specs=[row(HEADS * HEAD_PAD), row(BRANCH_W),
                   pl.BlockSpec((1, BRANCH_W, tm),
                                lambda i: (i // tiles_per_seq, 0, i % tiles_per_seq)),
                   row(LANE), row(BRANCH_W)],
        out_shape=[jax.ShapeDtypeStruct((T, HEADS * HEAD_PAD), BF16),
                   jax.ShapeDtypeStruct((T, BRANCH_W), BF16),
                   jax.ShapeDtypeStruct((batch, BRANCH_W, S), BF16),
                   jax.ShapeDtypeStruct((T, LANE), BF16),
                   jax.ShapeDtypeStruct((T, BRANCH_W), BF16)],
        compiler_params=_params(1),
        name="mla_in",
    )(x2, g, win, gq, gkv, wuq, wuk, wuvt, cc, ss)


def _flash_kernel(q_ref, kn_ref, kr_ref, vt_ref, o_ref,
                  st_ref, pt_ref, acc_ref, mask_ref, *, seq, tile):
    nq = seq // tile
    heads = range(HEADS_PER_STEP)

    kpos = lax.broadcasted_iota(jnp.int32, (tile, tile), 0)
    qpos = lax.broadcasted_iota(jnp.int32, (tile, tile), 1)
    mask_ref[...] = jnp.where(kpos <= qpos, 0.0, MASK_VALUE)

    def scores(hd, q0, k0, masked):
        q = q_ref[0, pl.ds(q0, tile), hd * HEAD_PAD:(hd + 1) * HEAD_PAD]
        k = jnp.concatenate([kn_ref[0, pl.ds(k0, tile), hd * LANE:(hd + 1) * LANE],
                             kr_ref[0, pl.ds(k0, tile), :]], axis=1)
        st = _dot_nt(k, q)
        if masked:
            st = st + mask_ref[...]
        st_ref[hd] = st
        return jnp.max(st, axis=0, keepdims=True)

    def softmax(hd, st, m, m_tile):
        m_new = jnp.maximum(m, m_tile)
        pt_ref[hd] = jnp.exp2(st - m_new).astype(BF16)
        return m_new, jnp.exp2(m - m_new)

    def accumulate(hd, k0, alpha):
        vt = vt_ref[0, hd * V_DIM:(hd + 1) * V_DIM, pl.ds(k0, tile)]
        vt = jnp.concatenate([vt, jnp.ones((SUM_ROWS, tile), BF16)], axis=0)
        acc_ref[hd] = alpha * acc_ref[hd] + _dot(vt, pt_ref[hd])

    def tile_start(i):
        return pl.multiple_of(i * tile, tile)

    def finalize(qi):
        q0 = tile_start(qi)
        for hd in heads:
            acc = acc_ref[hd]
            o = (acc[:V_DIM] / acc[V_DIM:V_DIM + 1]).T
            o_ref[0, pl.ds(q0, tile), hd * V_DIM:(hd + 1) * V_DIM] = o.astype(BF16)

    init = []
    for hd in heads:
        pt_ref[hd] = jnp.zeros((tile, tile), BF16)
        acc_ref[hd] = jnp.zeros((V_DIM + SUM_ROWS, tile), F32)
        init.append((jnp.full((1, tile), MASK_VALUE, F32), jnp.ones((1, tile), F32),
                     scores(hd, 0, 0, True)))

    def step(t, carry):
        qi, kj, stats = carry
        is_diag = kj == qi
        is_first = kj == 0
        k_prev = tile_start(jnp.maximum(jnp.where(is_first, qi - 1, kj - 1), 0))
        q_next = tile_start(jnp.where(is_diag, jnp.minimum(qi + 1, nq - 1), qi))
        k_next = tile_start(jnp.where(is_diag, 0, kj + 1))

        def body(next_masked):
            def run(stats):
                st = [st_ref[hd] for hd in heads]
                for hd in heads:
                    accumulate(hd, k_prev, stats[hd][1])
                m_next = [scores(hd, q_next, k_next, next_masked) for hd in heads]
                out = []
                for hd in heads:
                    m = jnp.where(is_first, MASK_VALUE, stats[hd][0])
                    out.append(softmax(hd, st[hd], m, stats[hd][2]) + (m_next[hd],))
                return tuple(out)
            return run

        stats = lax.cond(kj + 1 == qi, body(True), body(False), stats)

        @pl.when(jnp.logical_and(is_first, t > 0))
        def _():
            finalize(qi - 1)

        return (jnp.where(is_diag, qi + 1, qi), jnp.where(is_diag, 0, kj + 1), stats)

    _, _, stats = lax.fori_loop(0, nq * (nq + 1) // 2, step,
                                (jnp.int32(0), jnp.int32(0), tuple(init)))
    for hd in heads:
        accumulate(hd, tile_start(nq - 1), stats[hd][1])
    finalize(nq - 1)


def _flash(q, kn, kr, vt, tile):
    B, S, _ = q.shape
    n = HEADS_PER_STEP
    heads = lambda w: pl.BlockSpec((1, S, n * w), lambda b, h: (b, 0, h))
    return pl.pallas_call(
        functools.partial(_flash_kernel, seq=S, tile=tile),
        grid=(B, HEADS // n),
        in_specs=[heads(HEAD_PAD), heads(LANE),
                  pl.BlockSpec((1, S, LANE), lambda b, h: (b, 0, 0)),
                  pl.BlockSpec((1, n * V_DIM, S), lambda b, h: (b, h, 0))],
        out_specs=heads(V_DIM),
        out_shape=jax.ShapeDtypeStruct((B, S, BRANCH_W), BF16),
        scratch_shapes=[pltpu.VMEM((n, tile, tile), F32),
                        pltpu.VMEM((n, tile, tile), BF16),
                        pltpu.VMEM((n, V_DIM + SUM_ROWS, tile), F32),
                        pltpu.VMEM((tile, tile), F32)],
        compiler_params=_params(2),
        name="flash",
    )(q, kn, kr, vt)


def _attn_residual(x, attn_ref, sg_ref, w_ref):
    a = (attn_ref[...].astype(F32) * sg_ref[...].astype(F32)).astype(BF16)
    return x + _dot(a, w_ref[...])


def _out_proj_kernel(attn_ref, sg_ref, w_ref, x_ref, fg_ref, o_ref, *, final_norm):
    y = _attn_residual(x_ref[...], attn_ref, sg_ref, w_ref)
    if final_norm:
        y = _rms(y, fg_ref[...])
    o_ref[...] = y


def _out_proj(attn, sg, w, x2, fg, tm, final_norm):
    T = x2.shape[0]
    row = lambda width: pl.BlockSpec((tm, width), lambda i: (i, 0))
    return pl.pallas_call(
        functools.partial(_out_proj_kernel, final_norm=final_norm),
        grid=(T // tm,),
        in_specs=[row(BRANCH_W), row(BRANCH_W), _const_spec(w.shape), row(D_MODEL),
                  _const_spec(fg.shape)],
        out_specs=row(D_MODEL),
        out_shape=jax.ShapeDtypeStruct((T, D_MODEL), F32),
        compiler_params=_params(1),
        name="out_proj",
    )(attn, sg, w, x2, fg)


def _gelu(x):
    return 0.5 * x * (1.0 + lax.erf(x * np.float32(math.sqrt(0.5))))


def _sgu_kernel(x_ref, attn_ref, sg_ref, wattn_ref, g_ref, win_ref, lng_ref, lnb_ref,
                ws_ref, bs_ref, wo_ref, fg_ref, o_ref, vln_ref, ug_ref, prod_ref,
                *, tm, final_norm):
    x = _attn_residual(x_ref[...], attn_ref, sg_ref, wattn_ref)
    xn = _rms(x, g_ref[...]).astype(BF16)

    gv = _gelu(_dot(xn, win_ref[:, BRANCH_W:2 * BRANCH_W]))
    mu = jnp.mean(gv, axis=-1, keepdims=True)
    d = gv - mu
    var = jnp.mean(d * d, axis=-1, keepdims=True)
    vln_ref[...] = (d * lax.rsqrt(var + LN_EPS) * lng_ref[...] + lnb_ref[...]).astype(BF16)

    ug_ref[...] = _gelu(_dot(xn, win_ref[:, :BRANCH_W]))
    gate = _dot(xn, win_ref[:, 2 * BRANCH_W:])
    ug_ref[...] = ug_ref[...] * (gate * jax.nn.sigmoid(gate))

    nc = tm // CHUNK
    row = lax.broadcasted_iota(jnp.int32, (CHUNK, CHUNK), 0)
    col = lax.broadcasted_iota(jnp.int32, (CHUNK, CHUNK), 1)
    tri = col <= row
    for g in range(GROUPS):
        c0 = g * LANE
        wm = jnp.where(tri, ws_ref[g], 0.0).astype(BF16)
        rhs = jnp.concatenate(
            [vln_ref[c * CHUNK:(c + 1) * CHUNK, c0:c0 + LANE] for c in range(nc)], axis=1)
        res = _dot(wm, rhs)
        bias = bs_ref[:, c0:c0 + LANE]
        for c in range(nc):
            sv = res[:, c * LANE:(c + 1) * LANE] + bias
            r0 = c * CHUNK
            prod_ref[r0:r0 + CHUNK, c0:c0 + LANE] = (
                ug_ref[r0:r0 + CHUNK, c0:c0 + LANE] * sv).astype(BF16)

    y = x + _dot(prod_ref[...], wo_ref[...])
    if final_norm:
        y = _rms(y, fg_ref[...])
    o_ref[...] = y


def _sgu_layer(x2, attn, sg, wattn, g, win, lng, lnb, ws, bs_full, wo, fg, tm, final_norm):
    T = x2.shape[0]
    row = lambda width: pl.BlockSpec((tm, width), lambda i: (i, 0))
    return pl.pallas_call(
        functools.partial(_sgu_kernel, tm=tm, final_norm=final_norm),
        grid=(T // tm,),
        in_specs=[row(D_MODEL), row(BRANCH_W), row(BRANCH_W), _const_spec(wattn.shape),
                  _const_spec(g.shape), _const_spec(win.shape), _const_spec(lng.shape),
                  _const_spec(lnb.shape), _const_spec(ws.shape), _const_spec(bs_full.shape),
                  _const_spec(wo.shape), _const_spec(fg.shape)],
        out_specs=row(D_MODEL),
        out_shape=jax.ShapeDtypeStruct((T, D_MODEL), F32),
        scratch_shapes=[pltpu.VMEM((tm, BRANCH_W), BF16),
                        pltpu.VMEM((tm, BRANCH_W), F32),
                        pltpu.VMEM((tm, BRANCH_W), BF16)],
        compiler_params=_params(1),
        name="sgu_layer",
    )(x2, attn, sg, wattn, g, win, lng, lnb, ws, bs_full, wo, fg)


def _prep_mla_weights(w_in, w_uq, w_ukv):
    cq, ckv, kr, gate = jnp.split(w_in, [Q_LORA, Q_LORA + KV_LORA, Q_LORA + KV_LORA + ROPE], axis=1)
    win = jnp.concatenate(
        [gate, cq, ckv, kr, jnp.zeros((D_MODEL, LANE - ROPE), w_in.dtype)], axis=1).astype(BF16)
    wq = w_uq.reshape(Q_LORA, HEADS, NOPE + ROPE)
    wq = jnp.pad(wq, ((0, 0), (0, 0), (0, HEAD_PAD - NOPE - ROPE)))
    wq = wq.reshape(Q_LORA, HEADS * HEAD_PAD).astype(BF16)
    wkv = w_ukv.reshape(KV_LORA, HEADS, NOPE + V_DIM)
    wuk = wkv[:, :, :NOPE].reshape(KV_LORA, HEADS * NOPE).astype(BF16)
    wuvt = wkv[:, :, NOPE:].reshape(KV_LORA, HEADS * V_DIM).T.astype(BF16)
    return win, wq, wuk, wuvt


def kernel(x, positions, norm_g, final_g, mla_w_in, mla_q_norm_g, mla_kv_norm_g, mla_w_uq,
           mla_w_ukv, mla_w_o, sgu_w_in, sgu_ln_g, sgu_ln_b, sgu_w_s, sgu_b_s, sgu_w_o):
    B, S, D = x.shape
    T = B * S
    depth = norm_g.shape[0]
    assert D == D_MODEL and S % 512 == 0

    x2 = x.reshape(T, D)
    cc, ss = _rope_tables(positions, 512)
    fg = final_g[None, :]

    for i in range(depth):
        g = norm_g[i][None, :]
        j = i // 2
        last = i == depth - 1
        if i % 2 == 0:
            win, wq, wuk, wuvt = _prep_mla_weights(mla_w_in[j], mla_w_uq[j], mla_w_ukv[j])
            q, kn, vt, kr, sg = _mla_in(x2, g, win, mla_q_norm_g[j][None, :],
                                        mla_kv_norm_g[j][None, :], wq, wuk, wuvt, cc, ss, B, 256)
            shp = lambda a: a.reshape(B, S, a.shape[-1])
            attn = _flash(shp(q), shp(kn), shp(kr), vt, 512).reshape(T, BRANCH_W)
            wattn = mla_w_o[j].astype(BF16)
            if last:
                x2 = _out_proj(attn, sg, wattn, x2, fg, 512, True)
        else:
            bs_full = jnp.repeat(sgu_b_s[j].T, LANE, axis=1)
            x2 = _sgu_layer(x2, attn, sg, wattn, g, sgu_w_in[j].astype(BF16),
                            sgu_ln_g[j][None, :], sgu_ln_b[j][None, :], sgu_w_s[j], bs_full,
                            sgu_w_o[j].astype(BF16), fg, 256, last)
    return x2.reshape(B, S, D)
```

```python
import functools
import math

import jax
import jax.numpy as jnp
import numpy as np
from jax import lax
from jax.experimental import pallas as pl
from jax.experimental.pallas import tpu as pltpu

D_MODEL = 1024
BRANCH_W = 2 * D_MODEL
HEADS = 16
NOPE = 128
ROPE = 64
V_DIM = 128
Q_LORA = 384
KV_LORA = 256
ROPE_THETA = 10000.0
ATTN_SCALE = (NOPE + ROPE) ** -0.5
CHUNK = 128
GROUPS = 16
EPS = 1e-6
LN_EPS = 1e-5

LANE = 128
HEAD_PAD = 2 * LANE
HEADS_PER_STEP = 4
SUM_ROWS = 16
ROW_BLOCK = 256
Q_PRESCALE = ATTN_SCALE * math.log2(math.e)
MASK_VALUE = -1e30
VMEM_LIMIT = 56 * 1024 * 1024

F32 = jnp.float32
BF16 = jnp.bfloat16
NT_DIMS = (((1,), (1,)), ((), ()))


def _params(n_axes):
    return pltpu.CompilerParams(
        dimension_semantics=("parallel",) * n_axes,
        vmem_limit_bytes=VMEM_LIMIT)


def _const_spec(shape):
    nd = len(shape)
    return pl.BlockSpec(shape, lambda *_: (0,) * nd, pipeline_mode=pl.Buffered(1))


def _rms(x, g):
    return x * lax.rsqrt(jnp.mean(x * x, axis=-1, keepdims=True) + EPS) * g


def _silu(x):
    h = 0.5 * x
    return h + h * jnp.tanh(h)


def _dot(a, b):
    return jnp.dot(a, b, preferred_element_type=F32)


def _dot_nt(a, b):
    return lax.dot_general(a, b, NT_DIMS, preferred_element_type=F32)


def _rope_table_kernel(pos_ref, freq_ref, cc_ref, ss_ref):
    ang = pos_ref[...].astype(F32) * freq_ref[...]
    lane = lax.broadcasted_iota(jnp.int32, ang.shape, 1)
    cc_ref[...] = jnp.where(lane < ROPE, jnp.cos(ang), 0.0)
    sn = jnp.sin(ang)
    ss_ref[...] = jnp.where(lane < ROPE // 2, -sn, jnp.where(lane < ROPE, sn, 0.0))


def _rope_tables(positions, tm):
    T = positions.size
    inv_freq = 1.0 / (ROPE_THETA ** (jnp.arange(0, ROPE, 2, dtype=F32) / ROPE))
    freq = jnp.concatenate([inv_freq, inv_freq, jnp.zeros((LANE - ROPE,), F32)])[None, :]
    pos = positions.reshape(T, 1)
    return pl.pallas_call(
        _rope_table_kernel,
        grid=(T // tm,),
        in_specs=[pl.BlockSpec((tm, 1), lambda i: (i, 0)), _const_spec((1, LANE))],
        out_specs=[pl.BlockSpec((tm, LANE), lambda i: (i, 0))] * 2,
        out_shape=[jax.ShapeDtypeStruct((T, LANE), F32)] * 2,
        compiler_params=_params(1),
        name="rope_tables",
    )(pos, freq)


def _rope(r, cc, ss):
    swapped = pltpu.roll(r, LANE - ROPE // 2, axis=1) + pltpu.roll(r, ROPE // 2, axis=1)
    return r * cc + swapped * ss


_CQ0 = BRANCH_W
_CKV0 = _CQ0 + Q_LORA
_KR0 = _CKV0 + KV_LORA


def _mla_in_kernel(x_ref, g_ref, win_ref, gq_ref, gkv_ref, wuq_ref, wuk_ref, wuvt_ref,
                   cc_ref, ss_ref, q_ref, kn_ref, vt_ref, kr_ref, sg_ref,
                   xn_ref, cq_ref, ckv_ref, *, tm):
    blocks = [slice(r, r + ROW_BLOCK) for r in range(0, tm, ROW_BLOCK)]

    def norm_in(rs):
        xn_ref[rs, :] = _rms(x_ref[rs, :], g_ref[...]).astype(BF16)

    def gate_path(rs):
        sg_ref[rs, :] = _silu(_dot(xn_ref[rs, :], win_ref[:, :BRANCH_W])).astype(BF16)

    def latents(rs):
        h = _dot(xn_ref[rs, :], win_ref[:, _CQ0:])
        kr_ref[rs, :] = _rope(h[:, _KR0 - _CQ0:], cc_ref[rs, :], ss_ref[rs, :]).astype(BF16)
        cq_ref[rs, :] = _rms(h[:, :Q_LORA], gq_ref[...] * Q_PRESCALE).astype(BF16)
        ckv_ref[rs, :] = _rms(h[:, Q_LORA:_KR0 - _CQ0], gkv_ref[...]).astype(BF16)

    def queries(rs):
        q = _dot(cq_ref[rs, :], wuq_ref[...])
        cc = cc_ref[rs, :]
        ss = ss_ref[rs, :]
        for hd in range(HEADS):
            c0 = hd * HEAD_PAD
            q_ref[rs, c0:c0 + LANE] = q[:, c0:c0 + LANE].astype(BF16)
            q_ref[rs, c0 + LANE:c0 + HEAD_PAD] = _rope(
                q[:, c0 + LANE:c0 + HEAD_PAD], cc, ss).astype(BF16)

    def keys(rs):
        kn_ref[rs, :] = _dot(ckv_ref[rs, :], wuk_ref[...]).astype(BF16)

    def values(rs):
        vt_ref[0, :, rs] = _dot_nt(wuvt_ref[...], ckv_ref[rs, :]).astype(BF16)

    for phase in (norm_in, gate_path, latents, queries, keys, values):
        for rs in blocks:
            phase(rs)


def _mla_in(x2, g, win, gq, gkv, wuq, wuk, wuvt, cc, ss, batch, tm):
    T = x2.shape[0]
    S = T // batch
    tiles_per_seq = S // tm
    row = lambda w: pl.BlockSpec((tm, w), lambda i: (i, 0))
    return pl.pallas_call(
        functools.partial(_mla_in_kernel, tm=tm),
        grid=(T // tm,),
        in_specs=[row(D_MODEL), _const_spec(g.shape), _const_spec(win.shape),
                  _const_spec(gq.shape), _const_spec(gkv.shape),
                  _const_spec(wuq.shape), _const_spec(wuk.shape), _const_spec(wuvt.shape),
                  row(LANE), row(LANE)],
        out_specs=[row(HEADS * HEAD_PAD), row(BRANCH_W),
                   pl.BlockSpec((1, BRANCH_W, tm),
                                lambda i: (i // tiles_per_seq, 0, i % tiles_per_seq)),
                   row(LANE), row(BRANCH_W)],
        out_shape=[jax.ShapeDtypeStruct((T, HEADS * HEAD_PAD), BF16),
                   jax.ShapeDtypeStruct((T, BRANCH_W), BF16),
                   jax.ShapeDtypeStruct((batch, BRANCH_W, S), BF16),
                   jax.ShapeDtypeStruct((T, LANE), BF16),
                   jax.ShapeDtypeStruct((T, BRANCH_W), BF16)],
        scratch_shapes=[pltpu.VMEM((tm, D_MODEL), BF16),
                        pltpu.VMEM((tm, Q_LORA), BF16),
                        pltpu.VMEM((tm, KV_LORA), BF16)],
        compiler_params=_params(1),
        name="mla_in",
    )(x2, g, win, gq, gkv, wuq, wuk, wuvt, cc, ss)


def _flash_kernel(q_ref, kn_ref, kr_ref, vt_ref, o_ref,
                  st_ref, pt_ref, acc_ref, mask_ref, *, seq, tile):
    nq = seq // tile
    heads = range(HEADS_PER_STEP)

    kpos = lax.broadcasted_iota(jnp.int32, (tile, tile), 0)
    qpos = lax.broadcasted_iota(jnp.int32, (tile, tile), 1)
    mask_ref[...] = jnp.where(kpos <= qpos, 0.0, MASK_VALUE)

    def scores(hd, q0, k0, masked):
        q = q_ref[0, pl.ds(q0, tile), hd * HEAD_PAD:(hd + 1) * HEAD_PAD]
        k = jnp.concatenate([kn_ref[0, pl.ds(k0, tile), hd * LANE:(hd + 1) * LANE],
                             kr_ref[0, pl.ds(k0, tile), :]], axis=1)
        st = _dot_nt(k, q)
        if masked:
            st = st + mask_ref[...]
        st_ref[hd] = st
        return jnp.max(st, axis=0, keepdims=True)

    def softmax(hd, st, m, m_tile):
        m_new = jnp.maximum(m, m_tile)
        pt_ref[hd] = jnp.exp2(st - m_new).astype(BF16)
        return m_new, jnp.exp2(m - m_new)

    def accumulate(hd, k0, alpha):
        vt = vt_ref[0, hd * V_DIM:(hd + 1) * V_DIM, pl.ds(k0, tile)]
        vt = jnp.concatenate([vt, jnp.ones((SUM_ROWS, tile), BF16)], axis=0)
        acc_ref[hd] = alpha * acc_ref[hd] + _dot(vt, pt_ref[hd])

    def tile_start(i):
        return pl.multiple_of(i * tile, tile)

    def finalize(qi):
        q0 = tile_start(qi)
        for hd in heads:
            acc = acc_ref[hd]
            o = (acc[:V_DIM] / acc[V_DIM:V_DIM + 1]).T
            o_ref[0, pl.ds(q0, tile), hd * V_DIM:(hd + 1) * V_DIM] = o.astype(BF16)

    init = []
    for hd in heads:
        pt_ref[hd] = jnp.zeros((tile, tile), BF16)
        acc_ref[hd] = jnp.zeros((V_DIM + SUM_ROWS, tile), F32)
        init.append((jnp.full((1, tile), MASK_VALUE, F32), jnp.ones((1, tile), F32),
                     scores(hd, 0, 0, True)))

    def step(t, carry):
        qi, kj, stats = carry
        is_diag = kj == qi
        is_first = kj == 0
        k_prev = tile_start(jnp.maximum(jnp.where(is_first, qi - 1, kj - 1), 0))
        q_next = tile_start(jnp.where(is_diag, jnp.minimum(qi + 1, nq - 1), qi))
        k_next = tile_start(jnp.where(is_diag, 0, kj + 1))

        def body(next_masked):
            def run(stats):
                st = [st_ref[hd] for hd in heads]
                for hd in heads:
                    accumulate(hd, k_prev, stats[hd][1])
                m_next = [scores(hd, q_next, k_next, next_masked) for hd in heads]
                out = []
                for hd in heads:
                    m = jnp.where(is_first, MASK_VALUE, stats[hd][0])
                    out.append(softmax(hd, st[hd], m, stats[hd][2]) + (m_next[hd],))
                return tuple(out)
            return run

        stats = lax.cond(kj + 1 == qi, body(True), body(False), stats)

        @pl.when(jnp.logical_and(is_first, t > 0))
        def _():
            finalize(qi - 1)

        return (jnp.where(is_diag, qi + 1, qi), jnp.where(is_diag, 0, kj + 1), stats)

    _, _, stats = lax.fori_loop(0, nq * (nq + 1) // 2, step,
                                (jnp.int32(0), jnp.int32(0), tuple(init)))
    for hd in heads:
        accumulate(hd, tile_start(nq - 1), stats[hd][1])
    finalize(nq - 1)


def _flash(q, kn, kr, vt, tile):
    B, S, _ = q.shape
    n = HEADS_PER_STEP
    heads = lambda w: pl.BlockSpec((1, S, n * w), lambda b, h: (b, 0, h))
    return pl.pallas_call(
        functools.partial(_flash_kernel, seq=S, tile=tile),
        grid=(B, HEADS // n),
        in_specs=[heads(HEAD_PAD), heads(LANE),
                  pl.BlockSpec((1, S, LANE), lambda b, h: (b, 0, 0)),
                  pl.BlockSpec((1, n * V_DIM, S), lambda b, h: (b, h, 0))],
        out_specs=heads(V_DIM),
        out_shape=jax.ShapeDtypeStruct((B, S, BRANCH_W), BF16),
        scratch_shapes=[pltpu.VMEM((n, tile, tile), F32),
                        pltpu.VMEM((n, tile, tile), BF16),
                        pltpu.VMEM((n, V_DIM + SUM_ROWS, tile), F32),
                        pltpu.VMEM((tile, tile), F32)],
        compiler_params=_params(2),
        name="flash",
    )(q, kn, kr, vt)


def _attn_residual(x, attn, sg, w_ref):
    a = (attn.astype(F32) * sg.astype(F32)).astype(BF16)
    return x + _dot(a, w_ref[...])


def _out_proj_kernel(attn_ref, sg_ref, w_ref, x_ref, fg_ref, o_ref, *, final_norm):
    y = _attn_residual(x_ref[...], attn_ref[...], sg_ref[...], w_ref)
    if final_norm:
        y = _rms(y, fg_ref[...])
    o_ref[...] = y


def _out_proj(attn, sg, w, x2, fg, tm, final_norm):
    T = x2.shape[0]
    row = lambda width: pl.BlockSpec((tm, width), lambda i: (i, 0))
    return pl.pallas_call(
        functools.partial(_out_proj_kernel, final_norm=final_norm),
        grid=(T // tm,),
        in_specs=[row(BRANCH_W), row(BRANCH_W), _const_spec(w.shape), row(D_MODEL),
                  _const_spec(fg.shape)],
        out_specs=row(D_MODEL),
        out_shape=jax.ShapeDtypeStruct((T, D_MODEL), F32),
        compiler_params=_params(1),
        name="out_proj",
    )(attn, sg, w, x2, fg)


def _gelu(x):
    return 0.5 * x * (1.0 + lax.erf(x * np.float32(math.sqrt(0.5))))


def _sgu_kernel(x_ref, attn_ref, sg_ref, wattn_ref, g_ref, win_ref, lng_ref, lnb_ref,
                ws_ref, bs_ref, wo_ref, fg_ref, o_ref, xn_ref, vln_ref, ug_ref, prod_ref,
                *, tm, final_norm):
    blocks = [slice(r, r + ROW_BLOCK) for r in range(0, tm, ROW_BLOCK)]
    row = lax.broadcasted_iota(jnp.int32, (CHUNK, CHUNK), 0)
    col = lax.broadcasted_iota(jnp.int32, (CHUNK, CHUNK), 1)
    tri = col <= row

    def residual_in(rs):
        x = _attn_residual(x_ref[rs, :], attn_ref[rs, :], sg_ref[rs, :], wattn_ref)
        o_ref[rs, :] = x
        xn_ref[rs, :] = _rms(x, g_ref[...]).astype(BF16)

    def v_path(rs):
        gv = _gelu(_dot(xn_ref[rs, :], win_ref[:, BRANCH_W:2 * BRANCH_W]))
        mu = jnp.mean(gv, axis=-1, keepdims=True)
        d = gv - mu
        var = jnp.mean(d * d, axis=-1, keepdims=True)
        vln_ref[rs, :] = (d * lax.rsqrt(var + LN_EPS) * lng_ref[...]
                          + lnb_ref[...]).astype(BF16)

    def u_path(rs):
        ug_ref[rs, :] = _gelu(_dot(xn_ref[rs, :], win_ref[:, :BRANCH_W]))

    def gate_path(rs):
        gate = _dot(xn_ref[rs, :], win_ref[:, 2 * BRANCH_W:])
        ug_ref[rs, :] = ug_ref[rs, :] * _silu(gate)

    def spatial(rs):
        chunks = range(rs.start, rs.stop, CHUNK)
        for g in range(GROUPS):
            c0 = g * LANE
            wm = jnp.where(tri, ws_ref[g], 0.0).astype(BF16)
            rhs = jnp.concatenate([vln_ref[r0:r0 + CHUNK, c0:c0 + LANE] for r0 in chunks],
                                  axis=1)
            res = _dot(wm, rhs)
            bias = bs_ref[:, c0:c0 + LANE]
            for c, r0 in enumerate(chunks):
                sv = res[:, c * LANE:(c + 1) * LANE] + bias
                prod_ref[r0:r0 + CHUNK, c0:c0 + LANE] = (
                    ug_ref[r0:r0 + CHUNK, c0:c0 + LANE] * sv).astype(BF16)

    def residual_out(rs):
        y = o_ref[rs, :] + _dot(prod_ref[rs, :], wo_ref[...])
        if final_norm:
            y = _rms(y, fg_ref[...])
        o_ref[rs, :] = y

    for phase in (residual_in, v_path, u_path, gate_path, spatial, residual_out):
        for rs in blocks:
            phase(rs)


def _sgu_layer(x2, attn, sg, wattn, g, win, lng, lnb, ws, bs_full, wo, fg, tm, final_norm):
    T = x2.shape[0]
    row = lambda width: pl.BlockSpec((tm, width), lambda i: (i, 0))
    return pl.pallas_call(
        functools.partial(_sgu_kernel, tm=tm, final_norm=final_norm),
        grid=(T // tm,),
        in_specs=[row(D_MODEL), row(BRANCH_W), row(BRANCH_W), _const_spec(wattn.shape),
                  _const_spec(g.shape), _const_spec(win.shape), _const_spec(lng.shape),
                  _const_spec(lnb.shape), _const_spec(ws.shape), _const_spec(bs_full.shape),
                  _const_spec(wo.shape), _const_spec(fg.shape)],
        out_specs=row(D_MODEL),
        out_shape=jax.ShapeDtypeStruct((T, D_MODEL), F32),
        scratch_shapes=[pltpu.VMEM((tm, D_MODEL), BF16),
                        pltpu.VMEM((tm, BRANCH_W), BF16),
                        pltpu.VMEM((tm, BRANCH_W), F32),
                        pltpu.VMEM((tm, BRANCH_W), BF16)],
        compiler_params=_params(1),
        name="sgu_layer",
    )(x2, attn, sg, wattn, g, win, lng, lnb, ws, bs_full, wo, fg)


def _prep_mla_weights(w_in, w_uq, w_ukv):
    w_in, w_uq, w_ukv = (w.astype(BF16) for w in (w_in, w_uq, w_ukv))
    n_lat = Q_LORA + KV_LORA + ROPE
    win = jnp.concatenate([w_in[:, n_lat:], w_in[:, :n_lat],
                           jnp.zeros((D_MODEL, LANE - ROPE), BF16)], axis=1)
    wq = w_uq.reshape(Q_LORA, HEADS, NOPE + ROPE)
    wq = jnp.pad(wq, ((0, 0), (0, 0), (0, HEAD_PAD - NOPE - ROPE)))
    wq = wq.reshape(Q_LORA, HEADS * HEAD_PAD)
    wkv = w_ukv.reshape(KV_LORA, HEADS, NOPE + V_DIM)
    wuk = wkv[:, :, :NOPE].reshape(KV_LORA, HEADS * NOPE)
    wuvt = wkv[:, :, NOPE:].reshape(KV_LORA, HEADS * V_DIM).T
    return win, wq, wuk, wuvt


def kernel(x, positions, norm_g, final_g, mla_w_in, mla_q_norm_g, mla_kv_norm_g, mla_w_uq,
           mla_w_ukv, mla_w_o, sgu_w_in, sgu_ln_g, sgu_ln_b, sgu_w_s, sgu_b_s, sgu_w_o):
    B, S, D = x.shape
    T = B * S
    depth = norm_g.shape[0]
    assert D == D_MODEL and S % 512 == 0

    x2 = x.reshape(T, D)
    cc, ss = _rope_tables(positions, 512)
    fg = final_g[None, :]

    for i in range(depth):
        g = norm_g[i][None, :]
        j = i // 2
        last = i == depth - 1
        if i % 2 == 0:
            win, wq, wuk, wuvt = _prep_mla_weights(mla_w_in[j], mla_w_uq[j], mla_w_ukv[j])
            q, kn, vt, kr, sg = _mla_in(x2, g, win, mla_q_norm_g[j][None, :],
                                        mla_kv_norm_g[j][None, :], wq, wuk, wuvt, cc, ss, B, 512)
            shp = lambda a: a.reshape(B, S, a.shape[-1])
            attn = _flash(shp(q), shp(kn), shp(kr), vt, 512).reshape(T, BRANCH_W)
            wattn = mla_w_o[j].astype(BF16)
            if last:
                x2 = _out_proj(attn, sg, wattn, x2, fg, 512, True)
        else:
            bs_full = jnp.repeat(sgu_b_s[j].T, LANE, axis=1)
            x2 = _sgu_layer(x2, attn, sg, wattn, g, sgu_w_in[j].astype(BF16),
                            sgu_ln_g[j][None, :], sgu_ln_b[j][None, :], sgu_w_s[j], bs_full,
                            sgu_w_o[j].astype(BF16), fg, 512, last)
    return x2.reshape(B, S, D)
```

```python
import functools
import math

import jax
import jax.numpy as jnp
import numpy as np
from jax import lax
from jax.experimental import pallas as pl
from jax.experimental.pallas import tpu as pltpu

D_MODEL = 1024
BRANCH_W = 2 * D_MODEL
HEADS = 16
NOPE = 128
ROPE = 64
V_DIM = 128
Q_LORA = 384
KV_LORA = 256
ROPE_THETA = 10000.0
ATTN_SCALE = (NOPE + ROPE) ** -0.5
CHUNK = 128
GROUPS = 16
EPS = 1e-6
LN_EPS = 1e-5

LANE = 128
PAIR_WIDTH = 3 * LANE
TOKEN_TILE = 512
ATTN_TILE = 512
HEADS_PER_STEP = 4
SUM_ROWS = 16
ROW_BLOCK = 256
Q_PRESCALE = ATTN_SCALE * math.log2(math.e)
MASK_VALUE = -1e30
VMEM_LIMIT = 56 * 1024 * 1024

F32 = jnp.float32
BF16 = jnp.bfloat16
NT_DIMS = (((1,), (1,)), ((), ()))


def _params(n_axes):
    return pltpu.CompilerParams(
        dimension_semantics=("parallel",) * n_axes,
        vmem_limit_bytes=VMEM_LIMIT)


def _const_spec(shape):
    nd = len(shape)
    return pl.BlockSpec(shape, lambda *_: (0,) * nd, pipeline_mode=pl.Buffered(1))


def _rms(x, g):
    return x * lax.rsqrt(jnp.mean(x * x, axis=-1, keepdims=True) + EPS) * g


def _silu(x):
    h = 0.5 * x
    return h + h * jnp.tanh(h)


def _dot(a, b):
    return jnp.dot(a, b, preferred_element_type=F32)


def _dot_nt(a, b):
    return lax.dot_general(a, b, NT_DIMS, preferred_element_type=F32)


def _first_half(shape):
    return lax.broadcasted_iota(jnp.int32, shape, 1) % ROPE < ROPE // 2


def _rope_table_kernel(pos_ref, freq_ref, cc_ref, ss_ref):
    ang = pos_ref[...].astype(F32) * freq_ref[...]
    cc_ref[...] = jnp.cos(ang)
    sn = jnp.sin(ang)
    ss_ref[...] = jnp.where(_first_half(ang.shape), -sn, sn)


def _rope_tables(positions, tm):
    T = positions.size
    inv_freq = 1.0 / (ROPE_THETA ** (jnp.arange(0, ROPE, 2, dtype=F32) / ROPE))
    freq = jnp.tile(inv_freq, LANE // (ROPE // 2))[None, :]
    pos = positions.reshape(T, 1)
    return pl.pallas_call(
        _rope_table_kernel,
        grid=(T // tm,),
        in_specs=[pl.BlockSpec((tm, 1), lambda i: (i, 0)), _const_spec((1, LANE))],
        out_specs=[pl.BlockSpec((tm, LANE), lambda i: (i, 0))] * 2,
        out_shape=[jax.ShapeDtypeStruct((T, LANE), F32)] * 2,
        compiler_params=_params(1),
        name="rope_tables",
    )(pos, freq)


def _rope(r, cc, ss, first_half):
    swapped = jnp.where(first_half, pltpu.roll(r, LANE - ROPE // 2, axis=1),
                        pltpu.roll(r, ROPE // 2, axis=1))
    return r * cc + swapped * ss


_CQ0 = BRANCH_W
_CKV0 = _CQ0 + Q_LORA
_KR0 = _CKV0 + KV_LORA


def _mla_in_kernel(x_ref, g_ref, win_ref, gq_ref, gkv_ref, wuq_ref, wuk_ref, wuvt_ref,
                   cc_ref, ss_ref, q_ref, kn_ref, vt_ref, kr_ref, sg_ref,
                   xn_ref, cq_ref, ckv_ref, *, tm):
    blocks = [slice(r, r + ROW_BLOCK) for r in range(0, tm, ROW_BLOCK)]
    first_half = _first_half((ROW_BLOCK, LANE))

    def norm_in(rs):
        xn_ref[rs, :] = _rms(x_ref[rs, :], g_ref[...]).astype(BF16)

    def gate_path(rs):
        sg_ref[rs, :] = _silu(_dot(xn_ref[rs, :], win_ref[:, :BRANCH_W])).astype(BF16)

    def latents(rs):
        h = _dot(xn_ref[rs, :], win_ref[:, _CQ0:])
        kr = _rope(h[:, _KR0 - _CQ0:], cc_ref[rs, :], ss_ref[rs, :], first_half)
        kr_ref[rs, :LANE] = kr.astype(BF16)
        kr_ref[rs, LANE:] = pltpu.roll(kr, ROPE, axis=1).astype(BF16)
        cq_ref[rs, :] = _rms(h[:, :Q_LORA], gq_ref[...] * Q_PRESCALE).astype(BF16)
        ckv_ref[rs, :] = _rms(h[:, Q_LORA:_KR0 - _CQ0], gkv_ref[...]).astype(BF16)

    def queries(rs):
        q = _dot(cq_ref[rs, :], wuq_ref[...])
        cc = cc_ref[rs, :]
        ss = ss_ref[rs, :]
        for pair in range(HEADS // 2):
            c0 = pair * PAIR_WIDTH
            c1 = c0 + 2 * NOPE
            q_ref[rs, c0:c1] = q[:, c0:c1].astype(BF16)
            q_ref[rs, c1:c0 + PAIR_WIDTH] = _rope(
                q[:, c1:c0 + PAIR_WIDTH], cc, ss, first_half).astype(BF16)

    def keys(rs):
        kn_ref[rs, :] = _dot(ckv_ref[rs, :], wuk_ref[...]).astype(BF16)

    def values(rs):
        vt_ref[0, :, rs] = _dot_nt(wuvt_ref[...], ckv_ref[rs, :]).astype(BF16)

    for phase in (norm_in, gate_path, latents, queries, keys, values):
        for rs in blocks:
            phase(rs)


def _mla_in(x2, g, win, gq, gkv, wuq, wuk, wuvt, cc, ss, batch, tm):
    T = x2.shape[0]
    S = T // batch
    tiles_per_seq = S // tm
    row = lambda w: pl.BlockSpec((tm, w), lambda i: (i, 0))
    return pl.pallas_call(
        functools.partial(_mla_in_kernel, tm=tm),
        grid=(T // tm,),
        in_specs=[row(D_MODEL), _const_spec(g.shape), _const_spec(win.shape),
                  _const_spec(gq.shape), _const_spec(gkv.shape),
                  _const_spec(wuq.shape), _const_spec(wuk.shape), _const_spec(wuvt.shape),
                  row(LANE), row(LANE)],
        out_specs=[row(HEADS // 2 * PAIR_WIDTH), row(BRANCH_W),
                   pl.BlockSpec((1, BRANCH_W, tm),
                                lambda i: (i // tiles_per_seq, 0, i % tiles_per_seq)),
                   row(2 * LANE), row(BRANCH_W)],
        out_shape=[jax.ShapeDtypeStruct((T, HEADS // 2 * PAIR_WIDTH), BF16),
                   jax.ShapeDtypeStruct((T, BRANCH_W), BF16),
                   jax.ShapeDtypeStruct((batch, BRANCH_W, S), BF16),
                   jax.ShapeDtypeStruct((T, 2 * LANE), BF16),
                   jax.ShapeDtypeStruct((T, BRANCH_W), BF16)],
        scratch_shapes=[pltpu.VMEM((tm, D_MODEL), BF16),
                        pltpu.VMEM((tm, Q_LORA), BF16),
                        pltpu.VMEM((tm, KV_LORA), BF16)],
        compiler_params=_params(1),
        name="mla_in",
    )(x2, g, win, gq, gkv, wuq, wuk, wuvt, cc, ss)


def _flash_kernel(q_ref, kn_ref, kr_ref, vt_ref, o_ref,
                  st_ref, pt_ref, acc_ref, mask_ref, *, seq, tile):
    nq = seq // tile
    heads = range(HEADS_PER_STEP)

    kpos = lax.broadcasted_iota(jnp.int32, (tile, tile), 0)
    qpos = lax.broadcasted_iota(jnp.int32, (tile, tile), 1)
    mask_ref[...] = jnp.where(kpos <= qpos, 0.0, MASK_VALUE)

    def scores(hd, q0, k0, masked):
        pair0 = hd // 2 * PAIR_WIDTH
        rows = pl.ds(q0, tile)
        q = jnp.concatenate(
            [q_ref[0, rows, pair0 + hd % 2 * NOPE:pair0 + (hd % 2 + 1) * NOPE],
             q_ref[0, rows, pair0 + 2 * NOPE:pair0 + PAIR_WIDTH]], axis=1)
        rows = pl.ds(k0, tile)
        k = jnp.concatenate([kn_ref[0, rows, hd * NOPE:(hd + 1) * NOPE],
                             kr_ref[0, rows, hd % 2 * LANE:(hd % 2 + 1) * LANE]],
                            axis=1)
        st = _dot_nt(k, q)
        if masked:
            st = st + mask_ref[...]
        st_ref[hd] = st
        return jnp.max(st, axis=0, keepdims=True)

    def softmax(hd, st, m, m_tile):
        m_new = jnp.maximum(m, m_tile)
        pt_ref[hd] = jnp.exp2(st - m_new).astype(BF16)
        return m_new, jnp.exp2(m - m_new)

    def accumulate(hd, k0, alpha):
        vt = vt_ref[0, hd * V_DIM:(hd + 1) * V_DIM, pl.ds(k0, tile)]
        vt = jnp.concatenate([vt, jnp.ones((SUM_ROWS, tile), BF16)], axis=0)
        acc_ref[hd] = alpha * acc_ref[hd] + _dot(vt, pt_ref[hd])

    def tile_start(i):
        return pl.multiple_of(i * tile, tile)

    def finalize(qi):
        q0 = tile_start(qi)
        for hd in heads:
            acc = acc_ref[hd]
            o = (acc[:V_DIM] / acc[V_DIM:V_DIM + 1]).T
            o_ref[0, pl.ds(q0, tile), hd * V_DIM:(hd + 1) * V_DIM] = o.astype(BF16)

    init = []
    for hd in heads:
        pt_ref[hd] = jnp.zeros((tile, tile), BF16)
        acc_ref[hd] = jnp.zeros((V_DIM + SUM_ROWS, tile), F32)
        init.append((jnp.full((1, tile), MASK_VALUE, F32), jnp.ones((1, tile), F32),
                     scores(hd, 0, 0, True)))

    def step(t, carry):
        qi, kj, stats = carry
        is_diag = kj == qi
        is_first = kj == 0
        k_prev = tile_start(jnp.maximum(jnp.where(is_first, qi - 1, kj - 1), 0))
        q_next = tile_start(jnp.where(is_diag, jnp.minimum(qi + 1, nq - 1), qi))
        k_next = tile_start(jnp.where(is_diag, 0, kj + 1))

        def body(next_masked):
            def run(stats):
                st = [st_ref[hd] for hd in heads]
                for hd in heads:
                    accumulate(hd, k_prev, stats[hd][1])
                m_next = [scores(hd, q_next, k_next, next_masked) for hd in heads]
                out = []
                for hd in heads:
                    m = jnp.where(is_first, MASK_VALUE, stats[hd][0])
                    out.append(softmax(hd, st[hd], m, stats[hd][2]) + (m_next[hd],))
                return tuple(out)
            return run

        stats = lax.cond(kj + 1 == qi, body(True), body(False), stats)

        @pl.when(jnp.logical_and(is_first, t > 0))
        def _():
            finalize(qi - 1)

        return (jnp.where(is_diag, qi + 1, qi), jnp.where(is_diag, 0, kj + 1), stats)

    _, _, stats = lax.fori_loop(0, nq * (nq + 1) // 2, step,
                                (jnp.int32(0), jnp.int32(0), tuple(init)))
    for hd in heads:
        accumulate(hd, tile_start(nq - 1), stats[hd][1])
    finalize(nq - 1)


def _flash(q, kn, kr, vt, tile):
    B, S, _ = q.shape
    n = HEADS_PER_STEP
    heads = lambda w: pl.BlockSpec((1, S, n * w), lambda b, h: (b, 0, h))
    return pl.pallas_call(
        functools.partial(_flash_kernel, seq=S, tile=tile),
        grid=(B, HEADS // n),
        in_specs=[pl.BlockSpec((1, S, n // 2 * PAIR_WIDTH), lambda b, h: (b, 0, h)),
                  heads(NOPE),
                  pl.BlockSpec((1, S, 2 * LANE), lambda b, h: (b, 0, 0)),
                  pl.BlockSpec((1, n * V_DIM, S), lambda b, h: (b, h, 0))],
        out_specs=heads(V_DIM),
        out_shape=jax.ShapeDtypeStruct((B, S, BRANCH_W), BF16),
        scratch_shapes=[pltpu.VMEM((n, tile, tile), F32),
                        pltpu.VMEM((n, tile, tile), BF16),
                        pltpu.VMEM((n, V_DIM + SUM_ROWS, tile), F32),
                        pltpu.VMEM((tile, tile), F32)],
        compiler_params=_params(2),
        name="flash",
    )(q, kn, kr, vt)


def _attn_residual(x, attn, sg, w_ref):
    a = (attn.astype(F32) * sg.astype(F32)).astype(BF16)
    return x + _dot(a, w_ref[...])


def _out_proj_kernel(attn_ref, sg_ref, w_ref, x_ref, fg_ref, o_ref, *, final_norm):
    y = _attn_residual(x_ref[...], attn_ref[...], sg_ref[...], w_ref)
    if final_norm:
        y = _rms(y, fg_ref[...])
    o_ref[...] = y


def _out_proj(attn, sg, w, x2, fg, tm, final_norm):
    T = x2.shape[0]
    row = lambda width: pl.BlockSpec((tm, width), lambda i: (i, 0))
    return pl.pallas_call(
        functools.partial(_out_proj_kernel, final_norm=final_norm),
        grid=(T // tm,),
        in_specs=[row(BRANCH_W), row(BRANCH_W), _const_spec(w.shape), row(D_MODEL),
                  _const_spec(fg.shape)],
        out_specs=row(D_MODEL),
        out_shape=jax.ShapeDtypeStruct((T, D_MODEL), F32),
        compiler_params=_params(1),
        name="out_proj",
    )(attn, sg, w, x2, fg)


def _gelu(x):
    return 0.5 * x * (1.0 + lax.erf(x * np.float32(math.sqrt(0.5))))


def _sgu_kernel(x_ref, attn_ref, sg_ref, wattn_ref, g_ref, win_ref, lng_ref, lnb_ref,
                ws_ref, bs_ref, wo_ref, fg_ref, o_ref, xn_ref, vln_ref, ug_ref, prod_ref,
                *, tm, final_norm):
    blocks = [slice(r, r + ROW_BLOCK) for r in range(0, tm, ROW_BLOCK)]
    row = lax.broadcasted_iota(jnp.int32, (CHUNK, CHUNK), 0)
    col = lax.broadcasted_iota(jnp.int32, (CHUNK, CHUNK), 1)
    tri = col <= row

    def residual_in(rs):
        x = _attn_residual(x_ref[rs, :], attn_ref[rs, :], sg_ref[rs, :], wattn_ref)
        o_ref[rs, :] = x
        xn_ref[rs, :] = _rms(x, g_ref[...]).astype(BF16)

    def v_path(rs):
        gv = _gelu(_dot(xn_ref[rs, :], win_ref[:, BRANCH_W:2 * BRANCH_W]))
        mu = jnp.mean(gv, axis=-1, keepdims=True)
        d = gv - mu
        var = jnp.mean(d * d, axis=-1, keepdims=True)
        vln_ref[rs, :] = (d * lax.rsqrt(var + LN_EPS) * lng_ref[...]
                          + lnb_ref[...]).astype(BF16)

    def u_path(rs):
        ug_ref[rs, :] = _gelu(_dot(xn_ref[rs, :], win_ref[:, :BRANCH_W]))

    def gate_path(rs):
        gate = _dot(xn_ref[rs, :], win_ref[:, 2 * BRANCH_W:])
        ug_ref[rs, :] = ug_ref[rs, :] * _silu(gate)

    def spatial(rs):
        chunks = range(rs.start, rs.stop, CHUNK)
        for g in range(GROUPS):
            c0 = g * LANE
            wm = jnp.where(tri, ws_ref[g], 0.0).astype(BF16)
            rhs = jnp.concatenate([vln_ref[r0:r0 + CHUNK, c0:c0 + LANE] for r0 in chunks],
                                  axis=1)
            res = _dot(wm, rhs)
            bias = bs_ref[:, c0:c0 + LANE]
            for c, r0 in enumerate(chunks):
                sv = res[:, c * LANE:(c + 1) * LANE] + bias
                prod_ref[r0:r0 + CHUNK, c0:c0 + LANE] = (
                    ug_ref[r0:r0 + CHUNK, c0:c0 + LANE] * sv).astype(BF16)

    def residual_out(rs):
        y = o_ref[rs, :] + _dot(prod_ref[rs, :], wo_ref[...])
        if final_norm:
            y = _rms(y, fg_ref[...])
        o_ref[rs, :] = y

    for phase in (residual_in, v_path, u_path, gate_path, spatial, residual_out):
        for rs in blocks:
            phase(rs)


def _sgu_layer(x2, attn, sg, wattn, g, win, lng, lnb, ws, bs_full, wo, fg, tm, final_norm):
    T = x2.shape[0]
    row = lambda width: pl.BlockSpec((tm, width), lambda i: (i, 0))
    return pl.pallas_call(
        functools.partial(_sgu_kernel, tm=tm, final_norm=final_norm),
        grid=(T // tm,),
        in_specs=[row(D_MODEL), row(BRANCH_W), row(BRANCH_W), _const_spec(wattn.shape),
                  _const_spec(g.shape), _const_spec(win.shape), _const_spec(lng.shape),
                  _const_spec(lnb.shape), _const_spec(ws.shape), _const_spec(bs_full.shape),
                  _const_spec(wo.shape), _const_spec(fg.shape)],
        out_specs=row(D_MODEL),
        out_shape=jax.ShapeDtypeStruct((T, D_MODEL), F32),
        scratch_shapes=[pltpu.VMEM((tm, D_MODEL), BF16),
                        pltpu.VMEM((tm, BRANCH_W), BF16),
                        pltpu.VMEM((tm, BRANCH_W), F32),
                        pltpu.VMEM((tm, BRANCH_W), BF16)],
        compiler_params=_params(1),
        name="sgu_layer",
    )(x2, attn, sg, wattn, g, win, lng, lnb, ws, bs_full, wo, fg)


def _prep_mla_weights(w_in, w_uq, w_ukv):
    w_in, w_uq, w_ukv = (w.astype(BF16) for w in (w_in, w_uq, w_ukv))
    n_lat = Q_LORA + KV_LORA + ROPE
    win = jnp.concatenate([w_in[:, n_lat:], w_in[:, :n_lat],
                           jnp.zeros((D_MODEL, LANE - ROPE), BF16)], axis=1)
    wq = w_uq.reshape(Q_LORA, HEADS // 2, 2, NOPE + ROPE)
    wq = jnp.concatenate([wq[..., :NOPE].reshape(Q_LORA, HEADS // 2, 2 * NOPE),
                          wq[..., NOPE:].reshape(Q_LORA, HEADS // 2, 2 * ROPE)], axis=2)
    wq = wq.reshape(Q_LORA, HEADS // 2 * PAIR_WIDTH)
    wkv = w_ukv.reshape(KV_LORA, HEADS, NOPE + V_DIM)
    wuk = wkv[:, :, :NOPE].reshape(KV_LORA, HEADS * NOPE)
    wuvt = wkv[:, :, NOPE:].reshape(KV_LORA, HEADS * V_DIM).T
    return win, wq, wuk, wuvt


def kernel(x, positions, norm_g, final_g, mla_w_in, mla_q_norm_g, mla_kv_norm_g, mla_w_uq,
           mla_w_ukv, mla_w_o, sgu_w_in, sgu_ln_g, sgu_ln_b, sgu_w_s, sgu_b_s, sgu_w_o):
    B, S, D = x.shape
    T = B * S
    depth = norm_g.shape[0]
    assert D == D_MODEL and S % TOKEN_TILE == 0 and S % ATTN_TILE == 0

    x2 = x.reshape(T, D)
    cc, ss = _rope_tables(positions, TOKEN_TILE)
    fg = final_g[None, :]

    for i in range(depth):
        g = norm_g[i][None, :]
        j = i // 2
        last = i == depth - 1
        if i % 2 == 0:
            win, wq, wuk, wuvt = _prep_mla_weights(mla_w_in[j], mla_w_uq[j], mla_w_ukv[j])
            q, kn, vt, kr, sg = _mla_in(x2, g, win, mla_q_norm_g[j][None, :],
                                        mla_kv_norm_g[j][None, :], wq, wuk, wuvt, cc, ss, B,
                                        TOKEN_TILE)
            shp = lambda a: a.reshape(B, S, a.shape[-1])
            attn = _flash(shp(q), shp(kn), shp(kr), vt, ATTN_TILE).reshape(T, BRANCH_W)
            wattn = mla_w_o[j].astype(BF16)
            if last:
                x2 = _out_proj(attn, sg, wattn, x2, fg, TOKEN_TILE, True)
        else:
            bs_full = jnp.repeat(sgu_b_s[j].T, LANE, axis=1)
            x2 = _sgu_layer(x2, attn, sg, wattn, g, sgu_w_in[j].astype(BF16),
                            sgu_ln_g[j][None, :], sgu_ln_b[j][None, :], sgu_w_s[j], bs_full,
                            sgu_w_o[j].astype(BF16), fg, TOKEN_TILE, last)
    return x2.reshape(B, S, D)
```

```python
import functools
import math

import jax
import jax.numpy as jnp
import numpy as np
from jax import lax
from jax.experimental import pallas as pl
from jax.experimental.pallas import tpu as pltpu

D_MODEL = 1024
BRANCH_W = 2 * D_MODEL
HEADS = 16
NOPE = 128
ROPE = 64
V_DIM = 128
Q_LORA = 384
KV_LORA = 256
ROPE_THETA = 10000.0
ATTN_SCALE = (NOPE + ROPE) ** -0.5
CHUNK = 128
GROUPS = 16
EPS = 1e-6
LN_EPS = 1e-5

LANE = 128
PAIR_WIDTH = 3 * LANE
TOKEN_TILE = 512
ATTN_TILE = 512
HEADS_PER_STEP = 4
SUM_ROWS = 16
ROW_BLOCK = 256
Q_PRESCALE = ATTN_SCALE * math.log2(math.e)
MASK_VALUE = -1e30
VMEM_LIMIT = 56 * 1024 * 1024

F32 = jnp.float32
BF16 = jnp.bfloat16
NT_DIMS = (((1,), (1,)), ((), ()))


def _params(n_axes):
    return pltpu.CompilerParams(
        dimension_semantics=("parallel",) * n_axes,
        vmem_limit_bytes=VMEM_LIMIT)


def _const_spec(shape):
    nd = len(shape)
    return pl.BlockSpec(shape, lambda *_: (0,) * nd, pipeline_mode=pl.Buffered(1))


def _rms(x, g):
    return x * lax.rsqrt(jnp.mean(x * x, axis=-1, keepdims=True) + EPS) * g


def _silu(x):
    h = 0.5 * x
    return h + h * jnp.tanh(h)


def _dot(a, b):
    return jnp.dot(a, b, preferred_element_type=F32)


def _dot_nt(a, b):
    return lax.dot_general(a, b, NT_DIMS, preferred_element_type=F32)


def _first_half(shape):
    return lax.broadcasted_iota(jnp.int32, shape, 1) % ROPE < ROPE // 2


def _rope_table_kernel(pos_ref, freq_ref, cc_ref, ss_ref):
    ang = pos_ref[...].astype(F32) * freq_ref[...]
    cc_ref[...] = jnp.cos(ang)
    sn = jnp.sin(ang)
    ss_ref[...] = jnp.where(_first_half(ang.shape), -sn, sn)


def _rope_tables(positions, tm):
    T = positions.size
    inv_freq = 1.0 / (ROPE_THETA ** (jnp.arange(0, ROPE, 2, dtype=F32) / ROPE))
    freq = jnp.tile(inv_freq, LANE // (ROPE // 2))[None, :]
    pos = positions.reshape(T, 1)
    return pl.pallas_call(
        _rope_table_kernel,
        grid=(T // tm,),
        in_specs=[pl.BlockSpec((tm, 1), lambda i: (i, 0)), _const_spec((1, LANE))],
        out_specs=[pl.BlockSpec((tm, LANE), lambda i: (i, 0))] * 2,
        out_shape=[jax.ShapeDtypeStruct((T, LANE), F32)] * 2,
        compiler_params=_params(1),
        name="rope_tables",
    )(pos, freq)


def _rope(r, cc, ss, first_half):
    swapped = jnp.where(first_half, pltpu.roll(r, LANE - ROPE // 2, axis=1),
                        pltpu.roll(r, ROPE // 2, axis=1))
    return r * cc + swapped * ss


_CQ0 = BRANCH_W
_CKV0 = _CQ0 + Q_LORA
_KR0 = _CKV0 + KV_LORA


def _mla_in_kernel(x_ref, g_ref, win_ref, gq_ref, gkv_ref, wuq_ref, wuk_ref, wuvt_ref,
                   cc_ref, ss_ref, q_ref, kn_ref, vt_ref, kr_ref, sg_ref,
                   xn_ref, cq_ref, ckv_ref, *, tm):
    blocks = [slice(r, r + ROW_BLOCK) for r in range(0, tm, ROW_BLOCK)]
    first_half = _first_half((ROW_BLOCK, LANE))

    def norm_in(rs):
        xn_ref[rs, :] = _rms(x_ref[rs, :], g_ref[...]).astype(BF16)

    def gate_path(rs):
        sg_ref[rs, :] = _silu(_dot(xn_ref[rs, :], win_ref[:, :BRANCH_W])).astype(BF16)

    def latents(rs):
        h = _dot(xn_ref[rs, :], win_ref[:, _CQ0:])
        kr = _rope(h[:, _KR0 - _CQ0:], cc_ref[rs, :], ss_ref[rs, :], first_half)
        kr_ref[rs, :LANE] = kr.astype(BF16)
        kr_ref[rs, LANE:] = pltpu.roll(kr, ROPE, axis=1).astype(BF16)
        cq_ref[rs, :] = _rms(h[:, :Q_LORA], gq_ref[...] * Q_PRESCALE).astype(BF16)
        ckv_ref[rs, :] = _rms(h[:, Q_LORA:_KR0 - _CQ0], gkv_ref[...]).astype(BF16)

    def queries(rs):
        q = _dot(cq_ref[rs, :], wuq_ref[...])
        cc = cc_ref[rs, :]
        ss = ss_ref[rs, :]
        for pair in range(HEADS // 2):
            c0 = pair * PAIR_WIDTH
            c1 = c0 + 2 * NOPE
            q_ref[rs, c0:c1] = q[:, c0:c1].astype(BF16)
            q_ref[rs, c1:c0 + PAIR_WIDTH] = _rope(
                q[:, c1:c0 + PAIR_WIDTH], cc, ss, first_half).astype(BF16)

    def keys(rs):
        kn_ref[rs, :] = _dot(ckv_ref[rs, :], wuk_ref[...]).astype(BF16)

    def values(rs):
        vt_ref[0, :, rs] = _dot_nt(wuvt_ref[...], ckv_ref[rs, :]).astype(BF16)

    for phase in (norm_in, gate_path, latents, queries, keys, values):
        for rs in blocks:
            phase(rs)


def _mla_in(x2, g, win, gq, gkv, wuq, wuk, wuvt, cc, ss, batch, tm):
    T = x2.shape[0]
    S = T // batch
    tiles_per_seq = S // tm
    row = lambda w: pl.BlockSpec((tm, w), lambda i: (i, 0))
    return pl.pallas_call(
        functools.partial(_mla_in_kernel, tm=tm),
        grid=(T // tm,),
        in_specs=[row(D_MODEL), _const_spec(g.shape), _const_spec(win.shape),
                  _const_spec(gq.shape), _const_spec(gkv.shape),
                  _const_spec(wuq.shape), _const_spec(wuk.shape), _const_spec(wuvt.shape),
                  row(LANE), row(LANE)],
        out_specs=[row(HEADS // 2 * PAIR_WIDTH), row(BRANCH_W),
                   pl.BlockSpec((1, BRANCH_W, tm),
                                lambda i: (i // tiles_per_seq, 0, i % tiles_per_seq)),
                   row(2 * LANE), row(BRANCH_W)],
        out_shape=[jax.ShapeDtypeStruct((T, HEADS // 2 * PAIR_WIDTH), BF16),
                   jax.ShapeDtypeStruct((T, BRANCH_W), BF16),
                   jax.ShapeDtypeStruct((batch, BRANCH_W, S), BF16),
                   jax.ShapeDtypeStruct((T, 2 * LANE), BF16),
                   jax.ShapeDtypeStruct((T, BRANCH_W), BF16)],
        scratch_shapes=[pltpu.VMEM((tm, D_MODEL), BF16),
                        pltpu.VMEM((tm, Q_LORA), BF16),
                        pltpu.VMEM((tm, KV_LORA), BF16)],
        compiler_params=_params(1),
        name="mla_in",
    )(x2, g, win, gq, gkv, wuq, wuk, wuvt, cc, ss)


def _flash_kernel(q_ref, kn_ref, kr_ref, vt_ref, o_ref,
                  st_ref, pt_ref, acc_ref, mask_ref, *, seq, tile):
    nq = seq // tile
    heads = range(HEADS_PER_STEP)

    kpos = lax.broadcasted_iota(jnp.int32, (tile, tile), 0)
    qpos = lax.broadcasted_iota(jnp.int32, (tile, tile), 1)
    mask_ref[...] = jnp.where(kpos <= qpos, 0.0, MASK_VALUE)

    def scores(hd, q0, k0, masked):
        pair0 = hd // 2 * PAIR_WIDTH
        rows = pl.ds(q0, tile)
        q = jnp.concatenate(
            [q_ref[0, rows, pair0 + hd % 2 * NOPE:pair0 + (hd % 2 + 1) * NOPE],
             q_ref[0, rows, pair0 + 2 * NOPE:pair0 + PAIR_WIDTH]], axis=1)
        rows = pl.ds(k0, tile)
        k = jnp.concatenate([kn_ref[0, rows, hd * NOPE:(hd + 1) * NOPE],
                             kr_ref[0, rows, hd % 2 * LANE:(hd % 2 + 1) * LANE]],
                            axis=1)
        st = _dot_nt(k, q)
        if masked:
            st = st + mask_ref[...]
        st_ref[hd] = st
        return jnp.max(st, axis=0, keepdims=True)

    def softmax(hd, st, m, m_tile):
        m_new = jnp.maximum(m, m_tile)
        pt_ref[hd] = jnp.exp2(st - m_new).astype(BF16)
        return m_new, jnp.exp2(m - m_new)

    def accumulate(hd, k0, alpha):
        vt = vt_ref[0, hd * V_DIM:(hd + 1) * V_DIM, pl.ds(k0, tile)]
        vt = jnp.concatenate([vt, jnp.ones((SUM_ROWS, tile), BF16)], axis=0)
        acc_ref[hd] = alpha * acc_ref[hd] + _dot(vt, pt_ref[hd])

    def tile_start(i):
        return pl.multiple_of(i * tile, tile)

    def finalize(qi):
        q0 = tile_start(qi)
        for hd in heads:
            acc = acc_ref[hd]
            o = (acc[:V_DIM] / acc[V_DIM:V_DIM + 1]).T
            o_ref[0, pl.ds(q0, tile), hd * V_DIM:(hd + 1) * V_DIM] = o.astype(BF16)

    init = []
    for hd in heads:
        pt_ref[hd] = jnp.zeros((tile, tile), BF16)
        acc_ref[hd] = jnp.zeros((V_DIM + SUM_ROWS, tile), F32)
        init.append((jnp.full((1, tile), MASK_VALUE, F32), jnp.ones((1, tile), F32),
                     scores(hd, 0, 0, True)))

    def step(t, carry):
        qi, kj, stats = carry
        is_diag = kj == qi
        is_first = kj == 0
        k_prev = tile_start(jnp.maximum(jnp.where(is_first, qi - 1, kj - 1), 0))
        q_next = tile_start(jnp.where(is_diag, jnp.minimum(qi + 1, nq - 1), qi))
        k_next = tile_start(jnp.where(is_diag, 0, kj + 1))

        def body(next_masked):
            def run(stats):
                st = [st_ref[hd] for hd in heads]
                for hd in heads:
                    accumulate(hd, k_prev, stats[hd][1])
                m_next = [scores(hd, q_next, k_next, next_masked) for hd in heads]
                out = []
                for hd in heads:
                    m = jnp.where(is_first, MASK_VALUE, stats[hd][0])
                    out.append(softmax(hd, st[hd], m, stats[hd][2]) + (m_next[hd],))
                return tuple(out)
            return run

        stats = lax.cond(kj + 1 == qi, body(True), body(False), stats)

        @pl.when(jnp.logical_and(is_first, t > 0))
        def _():
            finalize(qi - 1)

        return (jnp.where(is_diag, qi + 1, qi), jnp.where(is_diag, 0, kj + 1), stats)

    _, _, stats = lax.fori_loop(0, nq * (nq + 1) // 2, step,
                                (jnp.int32(0), jnp.int32(0), tuple(init)))
    for hd in heads:
        accumulate(hd, tile_start(nq - 1), stats[hd][1])
    finalize(nq - 1)


def _flash(q, kn, kr, vt, tile):
    B, S, _ = q.shape
    n = HEADS_PER_STEP
    heads = lambda w: pl.BlockSpec((1, S, n * w), lambda b, h: (b, 0, h))
    return pl.pallas_call(
        functools.partial(_flash_kernel, seq=S, tile=tile),
        grid=(B, HEADS // n),
        in_specs=[pl.BlockSpec((1, S, n // 2 * PAIR_WIDTH), lambda b, h: (b, 0, h)),
                  heads(NOPE),
                  pl.BlockSpec((1, S, 2 * LANE), lambda b, h: (b, 0, 0)),
                  pl.BlockSpec((1, n * V_DIM, S), lambda b, h: (b, h, 0))],
        out_specs=heads(V_DIM),
        out_shape=jax.ShapeDtypeStruct((B, S, BRANCH_W), BF16),
        scratch_shapes=[pltpu.VMEM((n, tile, tile), F32),
                        pltpu.VMEM((n, tile, tile), BF16),
                        pltpu.VMEM((n, V_DIM + SUM_ROWS, tile), F32),
                        pltpu.VMEM((tile, tile), F32)],
        compiler_params=_params(2),
        name="flash",
    )(q, kn, kr, vt)


def _attn_residual(x, attn, sg, w_ref):
    a = (attn.astype(F32) * sg.astype(F32)).astype(BF16)
    return x + _dot(a, w_ref[...])


def _out_proj_kernel(attn_ref, sg_ref, w_ref, x_ref, fg_ref, o_ref, *, final_norm):
    y = _attn_residual(x_ref[...], attn_ref[...], sg_ref[...], w_ref)
    if final_norm:
        y = _rms(y, fg_ref[...])
    o_ref[...] = y


def _out_proj(attn, sg, w, x2, fg, tm, final_norm):
    T = x2.shape[0]
    row = lambda width: pl.BlockSpec((tm, width), lambda i: (i, 0))
    return pl.pallas_call(
        functools.partial(_out_proj_kernel, final_norm=final_norm),
        grid=(T // tm,),
        in_specs=[row(BRANCH_W), row(BRANCH_W), _const_spec(w.shape), row(D_MODEL),
                  _const_spec(fg.shape)],
        out_specs=row(D_MODEL),
        out_shape=jax.ShapeDtypeStruct((T, D_MODEL), F32),
        compiler_params=_params(1),
        name="out_proj",
    )(attn, sg, w, x2, fg)


def _gelu(x):
    return 0.5 * x * (1.0 + lax.erf(x * np.float32(math.sqrt(0.5))))


def _sgu_kernel(x_ref, attn_ref, sg_ref, wattn_ref, g_ref, win_ref, lng_ref, lnb_ref,
                ws_ref, bs_ref, wo_ref, fg_ref, o_ref, xn_ref, vln_ref, ug_ref, prod_ref,
                *, tm, final_norm):
    blocks = [slice(r, r + ROW_BLOCK) for r in range(0, tm, ROW_BLOCK)]
    row = lax.broadcasted_iota(jnp.int32, (CHUNK, CHUNK), 0)
    col = lax.broadcasted_iota(jnp.int32, (CHUNK, CHUNK), 1)
    tri = col <= row

    def residual_in(rs):
        x = _attn_residual(x_ref[rs, :], attn_ref[rs, :], sg_ref[rs, :], wattn_ref)
        o_ref[rs, :] = x
        xn_ref[rs, :] = _rms(x, g_ref[...]).astype(BF16)

    def v_path(rs):
        gv = _gelu(_dot(xn_ref[rs, :], win_ref[:, BRANCH_W:2 * BRANCH_W]))
        mu = jnp.mean(gv, axis=-1, keepdims=True)
        d = gv - mu
        var = jnp.mean(d * d, axis=-1, keepdims=True)
        vln_ref[rs, :] = (d * lax.rsqrt(var + LN_EPS) * lng_ref[...]
                          + lnb_ref[...]).astype(BF16)

    def u_path(rs):
        ug_ref[rs, :] = _gelu(_dot(xn_ref[rs, :], win_ref[:, :BRANCH_W]))

    def gate_path(rs):
        gate = _dot(xn_ref[rs, :], win_ref[:, 2 * BRANCH_W:])
        ug_ref[rs, :] = ug_ref[rs, :] * _silu(gate)

    def spatial(rs):
        chunks = range(rs.start, rs.stop, CHUNK)
        for g in range(GROUPS):
            c0 = g * LANE
            wm = jnp.where(tri, ws_ref[g], 0.0).astype(BF16)
            rhs = jnp.concatenate([vln_ref[r0:r0 + CHUNK, c0:c0 + LANE] for r0 in chunks],
                                  axis=1)
            res = _dot(wm, rhs)
            bias = bs_ref[:, c0:c0 + LANE]
            for c, r0 in enumerate(chunks):
                sv = res[:, c * LANE:(c + 1) * LANE] + bias
                prod_ref[r0:r0 + CHUNK, c0:c0 + LANE] = (
                    ug_ref[r0:r0 + CHUNK, c0:c0 + LANE] * sv).astype(BF16)

    def residual_out(rs):
        y = o_ref[rs, :] + _dot(prod_ref[rs, :], wo_ref[...])
        if final_norm:
            y = _rms(y, fg_ref[...])
        o_ref[rs, :] = y

    for phase in (residual_in, v_path, u_path, gate_path, spatial, residual_out):
        for rs in blocks:
            phase(rs)


def _sgu_layer(x2, attn, sg, wattn, g, win, lng, lnb, ws, bs_full, wo, fg, tm, final_norm):
    T = x2.shape[0]
    row = lambda width: pl.BlockSpec((tm, width), lambda i: (i, 0))
    return pl.pallas_call(
        functools.partial(_sgu_kernel, tm=tm, final_norm=final_norm),
        grid=(T // tm,),
        in_specs=[row(D_MODEL), row(BRANCH_W), row(BRANCH_W), _const_spec(wattn.shape),
                  _const_spec(g.shape), _const_spec(win.shape), _const_spec(lng.shape),
                  _const_spec(lnb.shape), _const_spec(ws.shape), _const_spec(bs_full.shape),
                  _const_spec(wo.shape), _const_spec(fg.shape)],
        out_specs=row(D_MODEL),
        out_shape=jax.ShapeDtypeStruct((T, D_MODEL), F32),
        scratch_shapes=[pltpu.VMEM((tm, D_MODEL), BF16),
                        pltpu.VMEM((tm, BRANCH_W), BF16),
                        pltpu.VMEM((tm, BRANCH_W), F32),
                        pltpu.VMEM((tm, BRANCH_W), BF16)],
        compiler_params=_params(1),
        name="sgu_layer",
    )(x2, attn, sg, wattn, g, win, lng, lnb, ws, bs_full, wo, fg)


def _prep_mla_weights(w_in, w_uq, w_ukv):
    w_in, w_uq, w_ukv = (w.astype(BF16) for w in (w_in, w_uq, w_ukv))
    n_lat = Q_LORA + KV_LORA + ROPE
    win = jnp.concatenate([w_in[:, n_lat:], w_in[:, :n_lat],
                           jnp.zeros((D_MODEL, LANE - ROPE), BF16)], axis=1)
    wq = w_uq.reshape(Q_LORA, HEADS // 2, 2, NOPE + ROPE)
    wq = jnp.concatenate([wq[..., :NOPE].reshape(Q_LORA, HEADS // 2, 2 * NOPE),
                          wq[..., NOPE:].reshape(Q_LORA, HEADS // 2, 2 * ROPE)], axis=2)
    wq = wq.reshape(Q_LORA, HEADS // 2 * PAIR_WIDTH)
    wkv = w_ukv.reshape(KV_LORA, HEADS, NOPE + V_DIM)
    wuk = wkv[:, :, :NOPE].reshape(KV_LORA, HEADS * NOPE)
    wuvt = wkv[:, :, NOPE:].reshape(KV_LORA, HEADS * V_DIM).T
    return win, wq, wuk, wuvt


def kernel(x, positions, norm_g, final_g, mla_w_in, mla_q_norm_g, mla_kv_norm_g, mla_w_uq,
           mla_w_ukv, mla_w_o, sgu_w_in, sgu_ln_g, sgu_ln_b, sgu_w_s, sgu_b_s, sgu_w_o):
    B, S, D = x.shape
    T = B * S
    depth = norm_g.shape[0]
    assert D == D_MODEL and S % TOKEN_TILE == 0 and S % ATTN_TILE == 0

    x2 = x.reshape(T, D)
    cc, ss = _rope_tables(positions, S)
    fg = final_g[None, :]

    for i in range(depth):
        g = norm_g[i][None, :]
        j = i // 2
        last = i == depth - 1
        if i % 2 == 0:
            win, wq, wuk, wuvt = _prep_mla_weights(mla_w_in[j], mla_w_uq[j], mla_w_ukv[j])
            q, kn, vt, kr, sg = _mla_in(x2, g, win, mla_q_norm_g[j][None, :],
                                        mla_kv_norm_g[j][None, :], wq, wuk, wuvt, cc, ss, B,
                                        TOKEN_TILE)
            shp = lambda a: a.reshape(B, S, a.shape[-1])
            attn = _flash(shp(q), shp(kn), shp(kr), vt, ATTN_TILE).reshape(T, BRANCH_W)
            wattn = mla_w_o[j].astype(BF16)
            if last:
                x2 = _out_proj(attn, sg, wattn, x2, fg, TOKEN_TILE, True)
        else:
            bs_full = jnp.repeat(sgu_b_s[j].T, LANE, axis=1)
            x2 = _sgu_layer(x2, attn, sg, wattn, g, sgu_w_in[j].astype(BF16),
                            sgu_ln_g[j][None, :], sgu_ln_b[j][None, :], sgu_w_s[j], bs_full,
                            sgu_w_o[j].astype(BF16), fg, TOKEN_TILE, last)
    return x2.reshape(B, S, D)
```
